```python
import math
import jax, jax.numpy as jnp
from jax import lax
import numpy as np

D_MODEL = 1024
BATCH = 8
SEQ = 4096
DEPTH = 1

HEAD_DIM = 64
NSA_HEADS = 8
NSA_KV_HEADS = 1
NSA_CMP_LEN = 32
NSA_CMP_STRIDE = 16
NSA_CMP_HIDDEN = 256
NSA_SEL_LEN = 64
NSA_SEL_TOPK = 16
NSA_WINDOW = 512
NSA_Q_CHUNK = 64
MOBA_HEADS = 8
MOBA_BLOCK = 256
MOBA_TOPK = 3
MOBA_Q_CHUNK = 32
MIX_WIDTH = (NSA_HEADS + MOBA_HEADS) * HEAD_DIM
IN_SIZES = (NSA_HEADS * HEAD_DIM,) + (NSA_KV_HEADS * HEAD_DIM,) * 6 + (NSA_HEADS * 3,) + (MOBA_HEADS * HEAD_DIM,) * 3
IN_COLS = sum(IN_SIZES)
N_EXPERTS = 256
TOP_K = 8
N_GROUPS = 8
TOPK_GROUPS = 4
D_EXPERT = 256
D_SHARED = 256
ROUTED_SCALE = 2.5
MOE_BLOCK = 128
ROPE_THETA = 10000.0
EPS = 1e-6
NEG = -1e30

kernel_name = "hybrid_nsa_moba_moe_adaln_block"


def rms_norm(x, g):
    xf = x.astype(jnp.float32)
    y = xf * lax.rsqrt(jnp.mean(xf * xf, axis=-1, keepdims=True) + EPS)
    return y * g.astype(jnp.float32)


def rope(t):
    S, dh = t.shape[-2], t.shape[-1]
    inv = ROPE_THETA ** (-jnp.arange(0, dh, 2, dtype=jnp.float32) / dh)
    ang = jnp.arange(S, dtype=jnp.float32)[:, None] * inv[None, :]
    cos, sin = jnp.cos(ang), jnp.sin(ang)
    t1, t2 = jnp.split(t.astype(jnp.float32), 2, axis=-1)
    return jnp.concatenate([t1 * cos - t2 * sin, t2 * cos + t1 * sin], axis=-1).astype(t.dtype)


def masked_softmax(logits, mask):
    l = jnp.where(mask, logits.astype(jnp.float32), NEG)
    return jax.nn.softmax(l, axis=-1) * mask


def split_heads(t, n):
    B, S = t.shape[:2]
    return t.reshape(B, S, n, HEAD_DIM).transpose(0, 2, 1, 3)


def nsa_mixer(q, k_cmp, v_cmp, k_slc, v_slc, k_win, v_win, gates,
              pos_k, w1_k, w2_k, pos_v, w1_v, w2_v):
    B, H, S, dh = q.shape
    G = NSA_KV_HEADS
    R = H // G
    C = NSA_Q_CHUNK
    W = NSA_WINDOW
    scale = dh ** -0.5
    n_cmp = (S - NSA_CMP_LEN) // NSA_CMP_STRIDE + 1
    n_sel = S // NSA_SEL_LEN
    n_top = min(NSA_SEL_TOPK, n_sel)

    cidx = jnp.arange(n_cmp)[:, None] * NSA_CMP_STRIDE + jnp.arange(NSA_CMP_LEN)[None, :]

    def compress(t, pos, w1, w2):
        blk = t[:, :, cidx] + pos
        flat = blk.reshape(B, G, n_cmp, NSA_CMP_LEN * dh)
        return jnp.dot(jax.nn.silu(jnp.dot(flat, w1)), w2)

    kc = compress(k_cmp, pos_k, w1_k, w2_k)
    vc = compress(v_cmp, pos_v, w1_v, w2_v)
    cmp_end = jnp.arange(n_cmp) * NSA_CMP_STRIDE + NSA_CMP_LEN - 1
    cs = jnp.arange(n_cmp)[:, None] * NSA_CMP_STRIDE
    ss = jnp.arange(n_sel)[None, :] * NSA_SEL_LEN
    overlap = ((cs < ss + NSA_SEL_LEN) & (cs + NSA_CMP_LEN > ss)).astype(jnp.float32)

    k_sb = k_slc.reshape(B, G, n_sel, NSA_SEL_LEN, dh)
    v_sb = v_slc.reshape(B, G, n_sel, NSA_SEL_LEN, dh)
    pad = ((0, 0), (0, 0), (W, 0), (0, 0))
    k_wp = jnp.pad(k_win, pad)
    v_wp = jnp.pad(v_win, pad)
    bi = jnp.arange(B)[:, None, None, None]
    gi = jnp.arange(G)[None, :, None, None]
    jsel = jnp.arange(n_sel)

    def chunk(ci):
        c0 = ci * C
        t = c0 + jnp.arange(C)
        qc = lax.dynamic_slice_in_dim(q, c0, C, axis=2).reshape(B, G, R, C, dh)
        s_c = jnp.einsum('bgrcd,bgnd->bgrcn', qc, kc) * scale
        p_c = masked_softmax(s_c, cmp_end[None, :] <= t[:, None])
        o_cmp = jnp.einsum('bgrcn,bgnd->bgrcd', p_c, vc)
        imp = jnp.einsum('bgrcn,nj->bgcj', p_c, overlap)
        cur = t // NSA_SEL_LEN
        valid = jsel[None, :] <= cur[:, None]
        forced = valid & ((jsel[None, :] == 0) | (jsel[None, :] == cur[:, None]) | (jsel[None, :] == cur[:, None] - 1))
        score = jnp.where(forced, jnp.inf, jnp.where(valid, imp, -jnp.inf))
        _, sel = lax.top_k(score, n_top)
        k_g = k_sb[bi, gi, sel]
        v_g = v_sb[bi, gi, sel]
        kpos = sel[..., None] * NSA_SEL_LEN + jnp.arange(NSA_SEL_LEN)
        smask = (kpos <= t[None, None, :, None, None]).reshape(B, G, 1, C, n_top * NSA_SEL_LEN)
        s_s = jnp.einsum('bgrcd,bgcnkd->bgrcnk', qc, k_g).reshape(B, G, R, C, n_top * NSA_SEL_LEN) * scale
        p_s = masked_softmax(s_s, smask)
        o_slc = jnp.einsum('bgrcm,bgcmd->bgrcd', p_s, v_g.reshape(B, G, C, n_top * NSA_SEL_LEN, dh))
        kw = lax.dynamic_slice_in_dim(k_wp, c0, C + W, axis=2)
        vw = lax.dynamic_slice_in_dim(v_wp, c0, C + W, axis=2)
        kp = c0 - W + jnp.arange(C + W)
        wmask = (kp[None, :] <= t[:, None]) & (kp[None, :] > t[:, None] - W) & (kp[None, :] >= 0)
        s_w = jnp.einsum('bgrcd,bgkd->bgrck', qc, kw) * scale
        o_win = jnp.einsum('bgrck,bgkd->bgrcd', masked_softmax(s_w, wmask), vw)
        gc = lax.dynamic_slice_in_dim(gates, c0, C, axis=2).reshape(B, G, R, C, 3)
        o = gc[..., 0:1] * o_cmp + gc[..., 1:2] * o_slc + gc[..., 2:3] * o_win
        return o.reshape(B, H, C, dh)

    out = lax.map(chunk, jnp.arange(S // C))
    return out.transpose(1, 0, 3, 2, 4).reshape(B, S, H * dh)


def moba_mixer(q, k, v):
    B, H, S, dh = q.shape
    C = MOBA_Q_CHUNK
    scale = dh ** -0.5
    nb = -(-S // MOBA_BLOCK)
    pad = ((0, 0), (0, 0), (0, nb * MOBA_BLOCK - S), (0, 0))
    k_b = jnp.pad(k, pad).reshape(B, H, nb, MOBA_BLOCK, dh)
    v_b = jnp.pad(v, pad).reshape(B, H, nb, MOBA_BLOCK, dh)
    k_mean = jnp.mean(k_b.astype(jnp.float32), axis=3)
    n_top = min(MOBA_TOPK, nb)
    bi = jnp.arange(B)[:, None, None, None]
    hi = jnp.arange(H)[None, :, None, None]
    blk_ids = jnp.arange(nb)

    def chunk(ci):
        c0 = ci * C
        t = c0 + jnp.arange(C)
        own = c0 // MOBA_BLOCK
        qc = lax.dynamic_slice_in_dim(q, c0, C, axis=2)
        k_own = lax.dynamic_index_in_dim(k_b, own, axis=2, keepdims=False)
        v_own = lax.dynamic_index_in_dim(v_b, own, axis=2, keepdims=False)
        own_mask = (own * MOBA_BLOCK + jnp.arange(MOBA_BLOCK))[None, :] <= t[:, None]
        gate = jnp.einsum('bhcd,bhnd->bhcn', qc.astype(jnp.float32), k_mean)
        _, sel = lax.top_k(jnp.where(blk_ids < own, gate, -jnp.inf), n_top)
        sel_ok = sel < own
        k_g = k_b[bi, hi, sel]
        v_g = v_b[bi, hi, sel]
        s_own = jnp.einsum('bhcd,bhkd->bhck', qc, k_own)
        s_past = jnp.einsum('bhcd,bhcnkd->bhcnk', qc, k_g).reshape(B, H, C, n_top * MOBA_BLOCK)
        logits = jnp.concatenate([s_own.astype(jnp.float32), s_past.astype(jnp.float32)], axis=-1) * scale
        mask = jnp.concatenate([jnp.broadcast_to(own_mask, (B, H, C, MOBA_BLOCK)),
                                jnp.repeat(sel_ok, MOBA_BLOCK, axis=-1)], axis=-1)
        p = masked_softmax(logits, mask)
        o = jnp.einsum('bhck,bhkd->bhcd', p[..., :MOBA_BLOCK], v_own) + jnp.einsum(
            'bhcm,bhcmd->bhcd', p[..., MOBA_BLOCK:], v_g.reshape(B, H, C, n_top * MOBA_BLOCK, dh))
        return o

    out = lax.map(chunk, jnp.arange(S // C))
    return out.transpose(1, 0, 3, 2, 4).reshape(B, S, H * dh)


def moe_ffn(h, w_router, router_bias, w_eg, w_eu, w_ed, w_sg, w_su, w_sd):
    T, D = h.shape
    scores = jax.nn.sigmoid(jnp.dot(h, w_router).astype(jnp.float32))
    biased = scores + router_bias.astype(jnp.float32)
    grp = biased.reshape(T, N_GROUPS, N_EXPERTS // N_GROUPS)
    grp_score = jnp.sum(lax.top_k(grp, 2)[0], axis=-1)
    _, gsel = lax.top_k(grp_score, TOPK_GROUPS)
    gmask = jnp.any(gsel[:, :, None] == jnp.arange(N_GROUPS)[None, None, :], axis=1)
    emask = jnp.repeat(gmask, N_EXPERTS // N_GROUPS, axis=1)
    _, eidx = lax.top_k(jnp.where(emask, biased, -jnp.inf), TOP_K)
    wts = jnp.take_along_axis(scores, eidx, axis=1)
    wts = wts / jnp.sum(wts, axis=-1, keepdims=True) * ROUTED_SCALE
    A = T * TOP_K
    e_flat = eidx.reshape(A)
    tok_flat = jnp.repeat(jnp.arange(T, dtype=jnp.int32), TOP_K)
    w_flat = wts.reshape(A)
    order = jnp.argsort(e_flat)
    e_s, tok_s, w_s = e_flat[order], tok_flat[order], w_flat[order]
    counts = jnp.bincount(e_flat, length=N_EXPERTS)
    starts = jnp.cumsum(counts) - counts
    padded = (counts + MOE_BLOCK - 1) // MOE_BLOCK * MOE_BLOCK
    pends = jnp.cumsum(padded)
    pstarts = pends - padded
    dest = pstarts[e_s] + jnp.arange(A) - starts[e_s]
    n_blk = -(-A // MOE_BLOCK) + N_EXPERTS
    buf_tok = jnp.zeros((n_blk * MOE_BLOCK,), jnp.int32).at[dest].set(tok_s)
    buf_w = jnp.zeros((n_blk * MOE_BLOCK,), jnp.float32).at[dest].set(w_s)
    blk_exp = jnp.minimum(jnp.searchsorted(pends, jnp.arange(n_blk) * MOE_BLOCK, side='right'), N_EXPERTS - 1)

    def body(acc, xs):
        tok, wt, e = xs
        xb = h[tok]
        y = jnp.dot(jax.nn.silu(jnp.dot(xb, w_eg[e])) * jnp.dot(xb, w_eu[e]), w_ed[e])
        return acc.at[tok].add(y.astype(jnp.float32) * wt[:, None]), None

    routed, _ = lax.scan(body, jnp.zeros((T, D), jnp.float32),
                         (buf_tok.reshape(n_blk, MOE_BLOCK), buf_w.reshape(n_blk, MOE_BLOCK), blk_exp))
    shared = jnp.dot(jax.nn.silu(jnp.dot(h, w_sg)) * jnp.dot(h, w_su), w_sd)
    return routed + shared


def setup_inputs(seed: int = 0) -> dict:
    key = jax.random.key(seed)
    ks = jax.random.split(key, 24)
    f32 = jnp.float32
    D, L, dh = D_MODEL, DEPTH, HEAD_DIM

    def nrm(k, shape, scale):
        return jax.random.normal(k, shape, f32) * scale

    return {
        "x": nrm(ks[0], (BATCH, SEQ, D), 1.0),
        "c": nrm(ks[1], (BATCH, D), 1.0),
        "w_ada": nrm(ks[2], (L, D, 6 * D), 0.5 * D ** -0.5),
        "b_ada": nrm(ks[3], (L, 6 * D), 0.02),
        "g_attn": 1.0 + nrm(ks[4], (L, D), 0.05),
        "w_in": nrm(ks[5], (L, D, IN_COLS), D ** -0.5),
        "cmp_pos_k": nrm(ks[6], (L, NSA_CMP_LEN, dh), 0.1),
        "cmp_w1_k": nrm(ks[7], (L, NSA_CMP_LEN * dh, NSA_CMP_HIDDEN), (NSA_CMP_LEN * dh) ** -0.5),
        "cmp_w2_k": nrm(ks[8], (L, NSA_CMP_HIDDEN, dh), NSA_CMP_HIDDEN ** -0.5),
        "cmp_pos_v": nrm(ks[9], (L, NSA_CMP_LEN, dh), 0.1),
        "cmp_w1_v": nrm(ks[10], (L, NSA_CMP_LEN * dh, NSA_CMP_HIDDEN), (NSA_CMP_LEN * dh) ** -0.5),
        "cmp_w2_v": nrm(ks[11], (L, NSA_CMP_HIDDEN, dh), NSA_CMP_HIDDEN ** -0.5),
        "w_out": nrm(ks[12], (L, MIX_WIDTH, D), MIX_WIDTH ** -0.5),
        "g_ffn": 1.0 + nrm(ks[13], (L, D), 0.05),
        "w_router": nrm(ks[14], (L, D, N_EXPERTS), D ** -0.5),
        "router_bias": nrm(ks[15], (L, N_EXPERTS), 0.01),
        "w_exp_gate": nrm(ks[16], (L, N_EXPERTS, D, D_EXPERT), D ** -0.5),
        "w_exp_up": nrm(ks[17], (L, N_EXPERTS, D, D_EXPERT), D ** -0.5),
        "w_exp_down": nrm(ks[18], (L, N_EXPERTS, D_EXPERT, D), D_EXPERT ** -0.5),
        "w_sh_gate": nrm(ks[19], (L, D, D_SHARED), D ** -0.5),
        "w_sh_up": nrm(ks[20], (L, D, D_SHARED), D ** -0.5),
        "w_sh_down": nrm(ks[21], (L, D_SHARED, D), D_SHARED ** -0.5),
        "g_final": 1.0 + nrm(ks[22], (D,), 0.05),
    }


def reference(x, c, w_ada, b_ada, g_attn, w_in, cmp_pos_k, cmp_w1_k, cmp_w2_k, cmp_pos_v, cmp_w1_v, cmp_w2_v,
              w_out, g_ffn, w_router, router_bias, w_exp_gate, w_exp_up, w_exp_down,
              w_sh_gate, w_sh_up, w_sh_down, g_final):
    B, S, D = x.shape
    G = NSA_KV_HEADS
    offs = np.cumsum(IN_SIZES)[:-1].tolist()
    cond = jax.nn.silu(c.astype(jnp.float32))
    h_res = x.astype(jnp.float32)
    for l in range(DEPTH):
        mod = jnp.dot(cond, w_ada[l]) + b_ada[l]
        sh1, sc1, gt1, sh2, sc2, gt2 = jnp.split(mod, 6, axis=-1)
        h = rms_norm(h_res, g_attn[l]) * (1.0 + sc1[:, None, :]) + sh1[:, None, :]
        proj = jnp.dot(h, w_in[l])
        qa, kc, vc, ksl, vsl, kw, vw, ga, qb, kb, vb = jnp.split(proj, offs, axis=-1)
        gates = jax.nn.sigmoid(ga.astype(jnp.float32)).reshape(B, S, NSA_HEADS, 3).transpose(0, 2, 1, 3)
        o_a = nsa_mixer(rope(split_heads(qa, NSA_HEADS)),
                        rope(split_heads(kc, G)), split_heads(vc, G),
                        rope(split_heads(ksl, G)), split_heads(vsl, G),
                        rope(split_heads(kw, G)), split_heads(vw, G), gates,
                        cmp_pos_k[l], cmp_w1_k[l], cmp_w2_k[l], cmp_pos_v[l], cmp_w1_v[l], cmp_w2_v[l])
        o_b = moba_mixer(rope(split_heads(qb, MOBA_HEADS)), rope(split_heads(kb, MOBA_HEADS)),
                         split_heads(vb, MOBA_HEADS))
        mix = jnp.concatenate([o_a, o_b], axis=-1)
        h_res = h_res + gt1[:, None, :] * jnp.dot(mix, w_out[l])
        h2 = rms_norm(h_res, g_ffn[l]) * (1.0 + sc2[:, None, :]) + sh2[:, None, :]
        y = moe_ffn(h2.reshape(B * S, D), w_router[l], router_bias[l], w_exp_gate[l], w_exp_up[l],
                    w_exp_down[l], w_sh_gate[l], w_sh_up[l], w_sh_down[l])
        h_res = h_res + gt2[:, None, :] * y.reshape(B, S, D)
    return rms_norm(h_res, g_final).astype(x.dtype)
```

```python
import functools
import math

import jax
import jax.numpy as jnp
import numpy as np
from jax import lax
from jax.experimental import pallas as pl
from jax.experimental.pallas import tpu as pltpu

F32 = jnp.float32
BF16 = jnp.bfloat16
I32 = jnp.int32

HEAD_DIM = 64
HALF = HEAD_DIM // 2
NSA_HEADS = 8
NSA_CMP_LEN = 32
NSA_CMP_STRIDE = 16
NSA_SEL_LEN = 64
NSA_SEL_TOPK = 16
NSA_WINDOW = 512
MOBA_HEADS = 8
MOBA_BLOCK = 256
MOBA_TOPK = 3
N_EXPERTS = 256
TOP_K = 8
N_GROUPS = 8
TOPK_GROUPS = 4
ROUTED_SCALE = 2.5
ROPE_THETA = 10000.0
EPS = 1e-6
NEG = -1e30

LANES = 128
SUBLANES = 8
ROW_TILE = (SUBLANES, LANES)
VMEM_LIMIT = 56 * 1024 * 1024
EXPERT_ROWS = 256
SEQ_TILE = 256


def _cparams(sem):
    return pltpu.CompilerParams(dimension_semantics=sem, vmem_limit_bytes=VMEM_LIMIT)


def _dot(a, b):
    return jnp.dot(a.astype(BF16), b.astype(BF16), preferred_element_type=F32)


def _dot_nt(a, b):
    return lax.dot_general(a.astype(BF16), b.astype(BF16), (((1,), (1,)), ((), ())),
                           preferred_element_type=F32)


def _dot_f32(a, b):
    return jnp.dot(a.astype(F32), b.astype(F32), preferred_element_type=F32)


def _silu(x):
    return x * jax.nn.sigmoid(x)


def _rms(x, g):
    return x * lax.rsqrt(jnp.mean(x * x, axis=-1, keepdims=True) + EPS) * g


def _ada_kernel(c_ref, w_ref, b_ref, o_ref):
    o_ref[...] = _dot_f32(_silu(c_ref[...]), w_ref[...]) + b_ref[...]


def _ada_mod(c, w, b):
    B, D = c.shape
    N = w.shape[1]
    tn = 1536 if N % 1536 == 0 else N
    return pl.pallas_call(
        _ada_kernel,
        grid=(N // tn,),
        in_specs=[pl.BlockSpec((B, D), lambda j: (0, 0)),
                  pl.BlockSpec((D, tn), lambda j: (0, j)),
                  pl.BlockSpec((1, tn), lambda j: (0, j))],
        out_specs=pl.BlockSpec((B, tn), lambda j: (0, j)),
        out_shape=jax.ShapeDtypeStruct((B, N), F32),
        compiler_params=_cparams(("arbitrary",)),
        name="ada_mod",
    )(c, w, b.reshape(1, N))


_OFF_QA, _OFF_KV, _OFF_G, _OFF_QB, _OFF_KB, _OFF_VB, _W_COLS = 0, 512, 896, 1024, 1536, 2048, 2560


def _swap_halves(t):
    n = t.shape[-1]
    lane = lax.broadcasted_iota(I32, t.shape, t.ndim - 1)
    left = pltpu.roll(t, n - HALF, t.ndim - 1)
    right = pltpu.roll(t, HALF, t.ndim - 1)
    return jnp.where((lane % HEAD_DIM) < HALF, left, right)


def _in_proj_kernel(x_ref, mod_ref, g_ref, w_ref, cq_ref, sq_ref, ckv_ref, skv_ref,
                    qa_ref, kc_ref, vc_ref, ks_ref, vs_ref, kw_ref, vw_ref, gt_ref,
                    qb_ref, kb_ref, vb_ref, km_ref):
    x = x_ref[0]
    sh1 = mod_ref[0, 0:1, :]
    sc1 = mod_ref[0, 1:2, :]
    h = _rms(x, g_ref[...]) * (1.0 + sc1) + sh1
    proj = _dot(h, w_ref[...])
    cq, sq = cq_ref[...], sq_ref[...]
    scale = HEAD_DIM ** -0.5

    def rope(t, c, s):
        return t * c + _swap_halves(t) * s

    qa = rope(proj[:, _OFF_QA:_OFF_QA + 512], cq, sq) * scale
    qa_ref[0] = qa.astype(BF16)
    kv = rope(proj[:, _OFF_KV:_OFF_KV + 384], ckv_ref[...], skv_ref[...])
    for j, ref in enumerate((kc_ref, vc_ref, ks_ref, vs_ref, kw_ref, vw_ref)):
        ref[0] = kv[:, HEAD_DIM * j:HEAD_DIM * (j + 1)].astype(BF16)
    gt_ref[0] = jax.nn.sigmoid(proj[:, _OFF_G:_OFF_G + LANES])
    qb = rope(proj[:, _OFF_QB:_OFF_QB + 512], cq, sq) * scale
    qb_ref[0] = qb.astype(BF16)
    kb = rope(proj[:, _OFF_KB:_OFF_KB + 512], cq, sq)
    kb_ref[0] = kb.astype(BF16)
    km_ref[0, 0] = jnp.mean(kb, axis=0, keepdims=True)
    vb_ref[0] = proj[:, _OFF_VB:_OFF_VB + 512].astype(BF16)


def _rope_tables(S):
    inv = ROPE_THETA ** (-jnp.arange(0, HEAD_DIM, 2, dtype=F32) / HEAD_DIM)
    ang = jnp.arange(S, dtype=F32)[:, None] * inv[None, :]
    cos, sin = jnp.cos(ang), jnp.sin(ang)
    c_head = jnp.concatenate([cos, cos], axis=-1)
    s_head = jnp.concatenate([-sin, sin], axis=-1)
    cq = jnp.tile(c_head, (1, 8))
    sq = jnp.tile(s_head, (1, 8))
    one, zero = jnp.ones_like(c_head), jnp.zeros_like(s_head)
    ckv = jnp.concatenate([c_head, one, c_head, one, c_head, one], axis=-1)
    skv = jnp.concatenate([s_head, zero, s_head, zero, s_head, zero], axis=-1)
    return cq, sq, ckv, skv


def _in_proj(x, mod3, g_attn, w_in):
    B, S, D = x.shape
    tm = SEQ_TILE
    nb = S // tm
    offs = np.cumsum((512, 64, 64, 64, 64, 64, 64, 24, 512, 512))
    cols = jnp.split(w_in, offs.tolist(), axis=1)
    qa_w, kv_w, g_w, qb_w, kb_w, vb_w = cols[0], jnp.concatenate(cols[1:7], axis=1), cols[7], cols[8], cols[9], cols[10]
    g_w = jnp.pad(g_w, ((0, 0), (0, LANES - g_w.shape[1])))
    w = jnp.concatenate([qa_w, kv_w, g_w, qb_w, kb_w, vb_w], axis=1).astype(BF16)
    cq, sq, ckv, skv = _rope_tables(S)
    tok = lambda w_: pl.BlockSpec((1, tm, w_), lambda b, i: (b, i, 0))
    tab = lambda w_: pl.BlockSpec((tm, w_), lambda b, i: (i, 0))
    sds = lambda w_, dt: jax.ShapeDtypeStruct((B, S, w_), dt)
    return pl.pallas_call(
        _in_proj_kernel,
        grid=(B, nb),
        in_specs=[tok(D),
                  pl.BlockSpec((1, 6, D), lambda b, i: (b, 0, 0)),
                  pl.BlockSpec((1, D), lambda b, i: (0, 0)),
                  pl.BlockSpec((D, _W_COLS), lambda b, i: (0, 0)),
                  tab(512), tab(512), tab(384), tab(384)],
        out_specs=[tok(512)] + [tok(HEAD_DIM)] * 6 + [tok(LANES), tok(512), tok(512), tok(512),
                   pl.BlockSpec((1, 1, 1, 512), lambda b, i: (b, i, 0, 0))],
        out_shape=[sds(512, BF16)] + [sds(HEAD_DIM, BF16)] * 6 + [sds(LANES, F32), sds(512, BF16), sds(512, BF16),
                   sds(512, BF16), jax.ShapeDtypeStruct((B, nb, 1, 512), F32)],
        compiler_params=_cparams(("parallel", "parallel")),
        name="in_proj",
    )(x, mod3, g_attn.reshape(1, D), w, cq, sq, ckv, skv)


def _compress_kernel(kr_ref, vr_ref, pk_ref, w1k_ref, w2k_ref, pv_ref, w1v_ref, w2v_ref, ko_ref, vo_ref):
    def one(r_ref, p_ref, w1_ref, w2_ref, o_ref):
        r = r_ref[0]
        half = r.shape[1]
        w1 = w1_ref[...]
        top = _dot(r, w1[:half])
        bot = _dot(r, w1[half:])
        bias = _dot_f32(jnp.broadcast_to(p_ref[...], (SUBLANES, 2 * half)), w1)[0:1]
        n = top.shape[0]
        pre = top + pltpu.roll(bot, n - 1, 0) + bias
        o_ref[0] = _dot(_silu(pre), w2_ref[...]).astype(BF16)

    one(kr_ref, pk_ref, w1k_ref, w2k_ref, ko_ref)
    one(vr_ref, pv_ref, w1v_ref, w2v_ref, vo_ref)


def _compress(kc, vc, pos_k, w1_k, w2_k, pos_v, w1_v, w2_v):
    B, S, dh = kc.shape
    ng = S // NSA_CMP_STRIDE
    width = NSA_CMP_STRIDE * dh
    kr = kc.reshape(B, ng, width)
    vr = vc.reshape(B, ng, width)
    hid = w1_k.shape[1]
    full = lambda a: pl.BlockSpec(a.shape, lambda b: (0,) * a.ndim)
    pk = pos_k.reshape(1, NSA_CMP_LEN * dh)
    pv = pos_v.reshape(1, NSA_CMP_LEN * dh)
    args = (kr, vr, pk, w1_k, w2_k, pv, w1_v, w2_v)
    return pl.pallas_call(
        _compress_kernel,
        grid=(B,),
        in_specs=[pl.BlockSpec((1, ng, width), lambda b: (b, 0, 0))] * 2 + [full(a) for a in args[2:]],
        out_specs=[pl.BlockSpec((1, ng, dh), lambda b: (b, 0, 0))] * 2,
        out_shape=[jax.ShapeDtypeStruct((B, ng, dh), BF16)] * 2,
        compiler_params=_cparams(("parallel",)),
        name="nsa_compress",
    )(*args)


def _rank_lt(score, k):
    n = score.shape[1]
    lane = lax.broadcasted_iota(I32, score.shape, 1)
    rank = jnp.zeros(score.shape, I32)
    for i in range(n):
        col = score[:, i:i + 1]
        beats = (col > score) | ((col == score) & (lane > i))
        rank = rank + beats.astype(I32)
    return rank < k


def _nsa_cmp_kernel(q_ref, kc_ref, vc_ref, o_ref, sel_ref, *, tq, n_sel):
    qi = pl.program_id(1)
    kc = kc_ref[0]
    vc = vc_ref[0]
    ng = kc.shape[0]
    t = qi * tq + lax.broadcasted_iota(I32, (tq, ng), 0)
    nidx = lax.broadcasted_iota(I32, (tq, ng), 1)
    cmask = (nidx * NSA_CMP_STRIDE + NSA_CMP_LEN - 1 <= t) & (nidx < ng - 1)
    psum = jnp.zeros((tq, ng), F32)
    outs = []
    for h in range(NSA_HEADS):
        q = q_ref[0, :, HEAD_DIM * h:HEAD_DIM * (h + 1)]
        s = jnp.where(cmask, _dot_nt(q, kc), NEG)
        m = jnp.max(s, axis=-1, keepdims=True)
        e = jnp.where(cmask, jnp.exp(s - m), 0.0)
        d = jnp.sum(e, axis=-1, keepdims=True)
        p = e / jnp.where(d > 0.0, d, 1.0)
        outs.append(_dot(p, vc))
        psum = psum + p
    o_ref[0] = jnp.concatenate(outs, axis=1)
    cs = lax.broadcasted_iota(I32, (ng, n_sel), 0) * NSA_CMP_STRIDE
    ss = lax.broadcasted_iota(I32, (ng, n_sel), 1) * NSA_SEL_LEN
    overlap = ((cs < ss + NSA_SEL_LEN) & (cs + NSA_CMP_LEN > ss)).astype(F32)
    imp = _dot_f32(psum, overlap)
    tj = qi * tq + lax.broadcasted_iota(I32, (tq, n_sel), 0)
    j = lax.broadcasted_iota(I32, (tq, n_sel), 1)
    cur = tj // NSA_SEL_LEN
    valid = j <= cur
    forced = valid & ((j == 0) | (j == cur) | (j == cur - 1))
    score = jnp.where(forced, jnp.inf, jnp.where(valid, imp, -jnp.inf))
    sel = _rank_lt(score, min(NSA_SEL_TOPK, n_sel)) & valid
    sel_ref[0] = sel.astype(F32)


def _nsa_cmp(qa, kcmp, vcmp):
    B, S, W = qa.shape
    tq = 128
    ng = kcmp.shape[1]
    n_sel = S // NSA_SEL_LEN
    return pl.pallas_call(
        functools.partial(_nsa_cmp_kernel, tq=tq, n_sel=n_sel),
        grid=(B, S // tq),
        in_specs=[pl.BlockSpec((1, tq, W), lambda b, i: (b, i, 0)),
                  pl.BlockSpec((1, ng, HEAD_DIM), lambda b, i: (b, 0, 0)),
                  pl.BlockSpec((1, ng, HEAD_DIM), lambda b, i: (b, 0, 0))],
        out_specs=[pl.BlockSpec((1, tq, W), lambda b, i: (b, i, 0)),
                   pl.BlockSpec((1, tq, n_sel), lambda b, i: (b, i, 0))],
        out_shape=[jax.ShapeDtypeStruct((B, S, W), F32), jax.ShapeDtypeStruct((B, S, n_sel), F32)],
        compiler_params=_cparams(("parallel", "parallel")),
        name="nsa_cmp_select",
    )(qa, kcmp, vcmp)


def _nsa_flash_kernel(*refs, mode, tq):
    if mode == "slc":
        q_ref, k_ref, v_ref, sel_ref, o_ref, qs_ref, m_ref, l_ref, acc_ref = refs
    else:
        q_ref, k_ref, v_ref, o_ref, qs_ref, m_ref, l_ref, acc_ref = refs
    H = NSA_HEADS
    tk = tq
    qi = pl.program_id(1)
    for h in range(H):
        qs_ref[h * tq:(h + 1) * tq, :] = q_ref[0, :, HEAD_DIM * h:HEAD_DIM * (h + 1)]
    m_ref[...] = jnp.full(m_ref.shape, NEG, F32)
    l_ref[...] = jnp.zeros(l_ref.shape, F32)
    acc_ref[...] = jnp.zeros(acc_ref.shape, F32)
    t_pos = qi * tq + lax.broadcasted_iota(I32, (tq, tk), 0)
    col = lax.broadcasted_iota(I32, (tq, tk), 1)

    def step(kt):
        k0 = pl.multiple_of(kt * tk, tk)
        k = k_ref[0, pl.ds(k0, tk), :]
        v = v_ref[0, pl.ds(k0, tk), :]
        s = _dot_nt(qs_ref[...], k).reshape(H, tq, tk)
        kpos = k0 + col
        if mode == "slc":
            n_sel = sel_ref.shape[2]
            blk = lax.broadcasted_iota(I32, (n_sel, tk), 0)
            kblk = (k0 + lax.broadcasted_iota(I32, (n_sel, tk), 1)) // NSA_SEL_LEN
            expand = (blk == kblk).astype(BF16)
            mask = (_dot(sel_ref[0], expand) > 0.5) & (kpos <= t_pos)
        else:
            mask = (kpos <= t_pos) & (kpos > t_pos - NSA_WINDOW)
        s = jnp.where(mask[None], s, NEG)
        m_prev = m_ref[...]
        m_new = jnp.maximum(m_prev, jnp.max(s, axis=-1, keepdims=True))
        alpha = jnp.exp(m_prev - m_new)
        p = jnp.exp(s - m_new)
        l_ref[...] = alpha * l_ref[...] + jnp.sum(p, axis=-1, keepdims=True)
        pv = _dot(p.reshape(H * tq, tk), v)
        acc_ref[...] = alpha.reshape(H * tq, 1) * acc_ref[...] + pv
        m_ref[...] = m_new

    step(qi)
    lo = 0 if mode == "slc" else jnp.maximum(qi - NSA_WINDOW // tk, 0)

    def body(kt, carry):
        step(kt)
        return carry

    lax.fori_loop(lo, qi, body, 0)
    o = acc_ref[...] / l_ref[...].reshape(H * tq, 1)
    o_ref[0] = jnp.concatenate([o[h * tq:(h + 1) * tq] for h in range(H)], axis=1)


def _nsa_flash(mode, qa, k, v, sel=None):
    B, S, W = qa.shape
    tq = SEQ_TILE
    H = NSA_HEADS
    in_specs = [pl.BlockSpec((1, tq, W), lambda b, i: (b, i, 0)),
                pl.BlockSpec((1, S, HEAD_DIM), lambda b, i: (b, 0, 0)),
                pl.BlockSpec((1, S, HEAD_DIM), lambda b, i: (b, 0, 0))]
    args = [qa, k, v]
    if mode == "slc":
        in_specs.append(pl.BlockSpec((1, tq, sel.shape[2]), lambda b, i: (b, i, 0)))
        args.append(sel)
    return pl.pallas_call(
        functools.partial(_nsa_flash_kernel, mode=mode, tq=tq),
        grid=(B, S // tq),
        in_specs=in_specs,
        out_specs=pl.BlockSpec((1, tq, W), lambda b, i: (b, i, 0)),
        out_shape=jax.ShapeDtypeStruct((B, S, W), F32),
        scratch_shapes=[pltpu.VMEM((H * tq, HEAD_DIM), BF16),
                        pltpu.VMEM((H, tq, 1), F32),
                        pltpu.VMEM((H, tq, 1), F32),
                        pltpu.VMEM((H * tq, HEAD_DIM), F32)],
        compiler_params=_cparams(("parallel", "parallel")),
        name="nsa_flash_" + mode,
    )(*args)


def _moba_kernel(q_ref, k_ref, v_ref, km_ref, o_ref, *, nb):
    tq = MOBA_BLOCK
    own = pl.program_id(2)
    row = lax.broadcasted_iota(I32, (tq, tq), 0)
    col = lax.broadcasted_iota(I32, (tq, tq), 1)
    blk = lax.broadcasted_iota(I32, (tq, nb), 1)
    outs = []
    for hh in range(2):
        sl = slice(HEAD_DIM * hh, HEAD_DIM * (hh + 1))
        q = q_ref[0, :, sl]
        gate = lax.dot_general(q.astype(F32), km_ref[0, :, sl], (((1,), (1,)), ((), ())),
                               preferred_element_type=F32)
        gate = jnp.where(blk < own, gate, -jnp.inf)
        sel = (_rank_lt(gate, min(MOBA_TOPK, nb)) & (blk < own)).astype(F32)
        own0 = pl.multiple_of(own * tq, tq)
        s = jnp.where(col <= row, _dot_nt(q, k_ref[0, pl.ds(own0, tq), sl]), NEG)
        m = jnp.max(s, axis=-1, keepdims=True)
        p = jnp.exp(s - m)
        l = jnp.sum(p, axis=-1, keepdims=True)
        acc = _dot(p, v_ref[0, pl.ds(own0, tq), sl])

        def body(n, carry, q=q, sel=sel, sl=sl):
            m, l, acc = carry
            n0 = pl.multiple_of(n * tq, tq)
            chosen = jnp.sum(jnp.where(blk == n, sel, 0.0), axis=-1, keepdims=True) > 0.5
            s = jnp.where(chosen, _dot_nt(q, k_ref[0, pl.ds(n0, tq), sl]), NEG)
            m_new = jnp.maximum(m, jnp.max(s, axis=-1, keepdims=True))
            alpha = jnp.exp(m - m_new)
            p = jnp.exp(s - m_new)
            l = alpha * l + jnp.sum(p, axis=-1, keepdims=True)
            acc = alpha * acc + _dot(p, v_ref[0, pl.ds(n0, tq), sl])
            return m_new, l, acc

        m, l, acc = lax.fori_loop(0, own, body, (m, l, acc))
        outs.append(acc / l)
    o_ref[0] = jnp.concatenate(outs, axis=1)


def _moba(qb, kb, vb, kmean):
    B, S, W = qb.shape
    tq = MOBA_BLOCK
    nb = S // tq
    pair = 2 * HEAD_DIM
    return pl.pallas_call(
        functools.partial(_moba_kernel, nb=nb),
        grid=(B, W // pair, nb),
        in_specs=[pl.BlockSpec((1, tq, pair), lambda b, h, i: (b, i, h)),
                  pl.BlockSpec((1, S, pair), lambda b, h, i: (b, 0, h)),
                  pl.BlockSpec((1, S, pair), lambda b, h, i: (b, 0, h)),
                  pl.BlockSpec((1, nb, pair), lambda b, h, i: (b, 0, h))],
        out_specs=pl.BlockSpec((1, tq, pair), lambda b, h, i: (b, i, h)),
        out_shape=jax.ShapeDtypeStruct((B, S, W), F32),
        compiler_params=_cparams(("parallel", "parallel", "parallel")),
        name="moba",
    )(qb, kb, vb, kmean)


def _post_attn_kernel(x_ref, mod_ref, oc_ref, os_ref, ow_ref, gt_ref, ob_ref, wo_ref, gf_ref, wr_ref, rb_ref,
                      wsg_ref, wsu_ref, wsd_ref, h2_ref, pre_ref, eidx_ref, wts_ref, hot_ref, cnt_ref):
    first = (pl.program_id(0) == 0) & (pl.program_id(1) == 0)
    x = x_ref[0]
    tm = x.shape[0]
    gt1 = mod_ref[0, 2:3, :]
    sh2 = mod_ref[0, 3:4, :]
    sc2 = mod_ref[0, 4:5, :]
    gt2 = mod_ref[0, 5:6, :]
    gates = gt_ref[0]
    gi = lax.broadcasted_iota(I32, (LANES, 512), 0)
    hi = lax.broadcasted_iota(I32, (LANES, 512), 1) // HEAD_DIM
    o_a = jnp.zeros((tm, 512), F32)
    for jbr, ref in enumerate((oc_ref, os_ref, ow_ref)):
        expand = (gi == 3 * hi + jbr).astype(F32)
        o_a = o_a + _dot_f32(gates, expand) * ref[0]
    wo = wo_ref[...]
    attn = _dot(o_a, wo[:512]) + _dot(ob_ref[0], wo[512:])
    h1 = x + gt1 * attn
    h2 = _rms(h1, gf_ref[...]) * (1.0 + sc2) + sh2
    h2_ref[0] = h2
    shared = _dot(_silu(_dot(h2, wsg_ref[...])) * _dot(h2, wsu_ref[...]), wsd_ref[...])
    pre_ref[0] = h1 + gt2 * shared
    scores = jax.nn.sigmoid(_dot(h2, wr_ref[...]))
    biased = scores + rb_ref[...]
    E = N_EXPERTS
    per = E // N_GROUPS
    lane = lax.broadcasted_iota(I32, (tm, E), 1)
    grp = lane // per
    gs = []
    for g in range(N_GROUPS):
        vals = jnp.where(grp == g, biased, -jnp.inf)
        m1 = jnp.max(vals, axis=-1, keepdims=True)
        i1 = jnp.min(jnp.where(vals == m1, lane, E), axis=-1, keepdims=True)
        m2 = jnp.max(jnp.where(lane == i1, -jnp.inf, vals), axis=-1, keepdims=True)
        gs.append(m1 + m2)
    emask = jnp.zeros((tm, E), jnp.bool_)
    for g in range(N_GROUPS):
        rank = jnp.zeros((tm, 1), I32)
        for i in range(N_GROUPS):
            beats = (gs[i] > gs[g]) | ((gs[i] == gs[g]) & (i < g))
            rank = rank + beats.astype(I32)
        emask = emask | ((rank < TOPK_GROUPS) & (grp == g))
    cand = jnp.where(emask, biased, -jnp.inf)
    lane128 = lax.broadcasted_iota(I32, (tm, LANES), 1)
    eidx = jnp.zeros((tm, LANES), I32)
    wsel = jnp.zeros((tm, LANES), F32)
    hot = jnp.zeros((tm, E), F32)
    wsum = jnp.zeros((tm, 1), F32)
    for kk in range(TOP_K):
        m = jnp.max(cand, axis=-1, keepdims=True)
        idx = jnp.min(jnp.where(cand == m, lane, E), axis=-1, keepdims=True)
        pick = lane == idx
        w = jnp.sum(jnp.where(pick, scores, 0.0), axis=-1, keepdims=True)
        cand = jnp.where(pick, -jnp.inf, cand)
        hot = hot + pick.astype(F32)
        eidx = jnp.where(lane128 == kk, idx, eidx)
        wsel = jnp.where(lane128 == kk, w, wsel)
        wsum = wsum + w
    eidx_ref[0] = eidx
    wts_ref[0] = wsel / wsum * ROUTED_SCALE
    hot_ref[0] = hot.astype(BF16)

    @pl.when(first)
    def _():
        cnt_ref[...] = jnp.zeros(cnt_ref.shape, F32)

    cnt_ref[...] += jnp.sum(hot, axis=0, keepdims=True)


def _post_attn(x, mod3, o_cmp, o_slc, o_win, gates, o_b, w_out, g_ffn, w_router, router_bias, w_sg, w_su, w_sd):
    B, S, D = x.shape
    tm = SEQ_TILE
    E = N_EXPERTS
    tok = lambda w_: pl.BlockSpec((1, tm, w_), lambda b, i: (b, i, 0))
    full = lambda a: pl.BlockSpec(a.shape, lambda b, i: (0,) * a.ndim)
    sds = lambda w_, dt: jax.ShapeDtypeStruct((B, S, w_), dt)
    consts = (w_out.astype(BF16), g_ffn.reshape(1, D), w_router.astype(BF16), router_bias.reshape(1, E),
              w_sg.astype(BF16), w_su.astype(BF16), w_sd.astype(BF16))
    return pl.pallas_call(
        _post_attn_kernel,
        grid=(B, S // tm),
        in_specs=[tok(D), pl.BlockSpec((1, 6, D), lambda b, i: (b, 0, 0)),
                  tok(512), tok(512), tok(512), tok(LANES), tok(512)] + [full(a) for a in consts],
        out_specs=[tok(D), tok(D), tok(LANES), tok(LANES), tok(E), pl.BlockSpec((1, E), lambda b, i: (0, 0))],
        out_shape=[sds(D, F32), sds(D, F32), sds(LANES, I32), sds(LANES, F32), sds(E, BF16),
                   jax.ShapeDtypeStruct((1, E), F32)],
        compiler_params=_cparams(("arbitrary", "arbitrary")),
        name="post_attn_router",
    )(x, mod3, o_cmp, o_slc, o_win, gates, o_b, *consts)


def _moe_pos_kernel(hot_ref, eidx_ref, start_ref, dest_ref, carry_ref):
    @pl.when(pl.program_id(0) == 0)
    def _():
        carry_ref[...] = start_ref[...]

    hot = hot_ref[...]
    tp, E = hot.shape
    r = lax.broadcasted_iota(I32, (tp, tp), 0)
    c = lax.broadcasted_iota(I32, (tp, tp), 1)
    lower = (c < r).astype(BF16)
    pos = carry_ref[...] + _dot(lower, hot)
    lane = lax.broadcasted_iota(I32, (tp, E), 1)
    lane128 = lax.broadcasted_iota(I32, (tp, LANES), 1)
    eidx = eidx_ref[...]
    dest = jnp.zeros((tp, LANES), I32)
    for kk in range(TOP_K):
        e = eidx[:, kk:kk + 1]
        d = jnp.sum(jnp.where(lane == e, pos, 0.0), axis=-1, keepdims=True)
        dest = jnp.where(lane128 == kk, d.astype(I32), dest)
    dest_ref[...] = dest
    carry_ref[...] += jnp.sum(hot.astype(F32), axis=0, keepdims=True)


def _moe_pos(hot, eidx, starts):
    T, E = hot.shape
    tp = 512 if T % 512 == 0 else SEQ_TILE
    return pl.pallas_call(
        _moe_pos_kernel,
        grid=(T // tp,),
        in_specs=[pl.BlockSpec((tp, E), lambda i: (i, 0)),
                  pl.BlockSpec((tp, LANES), lambda i: (i, 0)),
                  pl.BlockSpec((1, E), lambda i: (0, 0))],
        out_specs=pl.BlockSpec((tp, LANES), lambda i: (i, 0)),
        out_shape=jax.ShapeDtypeStruct((T, LANES), I32),
        scratch_shapes=[pltpu.VMEM((1, E), F32)],
        compiler_params=_cparams(("arbitrary",)),
        name="moe_pos",
    )(hot, eidx, starts)


def _moe_scatter_kernel(dest_ref, h_ref, zero_ref, xs_ref, sem):
    del zero_ref
    ts = h_ref.shape[0]

    def copy(r, kk):
        return pltpu.make_async_copy(h_ref.at[r], xs_ref.at[dest_ref[r * TOP_K + kk]], sem)

    def issue(r, carry):
        for kk in range(TOP_K):
            copy(r, kk).start()
        return carry

    def drain(r, carry):
        for kk in range(TOP_K):
            copy(r, kk).wait()
        return carry

    lax.fori_loop(0, ts, issue, 0)
    lax.fori_loop(0, ts, drain, 0)


def _moe_scatter(h3, dest_flat, n_rows):
    T = h3.shape[0]
    ts = 128
    zeros = jnp.zeros((n_rows,) + ROW_TILE, F32)
    return pl.pallas_call(
        _moe_scatter_kernel,
        grid=(T // ts,),
        in_specs=[pl.BlockSpec((ts * TOP_K,), lambda i: (i,), memory_space=pltpu.SMEM),
                  pl.BlockSpec((ts,) + ROW_TILE, lambda i: (i, 0, 0)),
                  pl.BlockSpec(memory_space=pl.ANY)],
        out_specs=pl.BlockSpec(memory_space=pl.ANY),
        out_shape=jax.ShapeDtypeStruct((n_rows,) + ROW_TILE, F32),
        scratch_shapes=[pltpu.SemaphoreType.DMA(())],
        input_output_aliases={2: 0},
        compiler_params=_cparams(("arbitrary",)),
        name="moe_scatter",
    )(dest_flat, h3, zeros)


def _moe_experts_kernel(be_ref, bf_ref, xs_ref, wg_ref, wu_ref, wd_ref, ys_ref, wg_s, wu_s, wd_s):
    i = pl.program_id(0)

    @pl.when(bf_ref[i] == 1)
    def _():
        wg_s[...] = wg_ref[0].astype(BF16)
        wu_s[...] = wu_ref[0].astype(BF16)
        wd_s[...] = wd_ref[0].astype(BF16)

    n_sub = xs_ref.shape[1]
    x = jnp.concatenate([xs_ref[:, s, :] for s in range(n_sub)], axis=1)
    a = _silu(_dot(x, wg_s[...])) * _dot(x, wu_s[...])
    y = _dot(a, wd_s[...])
    for s in range(n_sub):
        ys_ref[:, s, :] = y[:, LANES * s:LANES * (s + 1)]


def _moe_experts(xs, blk_exp, blk_first, w_eg, w_eu, w_ed):
    n_rows = xs.shape[0]
    E, D, De = w_eg.shape
    n_blk = n_rows // EXPERT_ROWS
    grid_spec = pltpu.PrefetchScalarGridSpec(
        num_scalar_prefetch=2,
        grid=(n_blk,),
        in_specs=[pl.BlockSpec((EXPERT_ROWS,) + ROW_TILE, lambda i, be, bf: (i, 0, 0)),
                  pl.BlockSpec((1, D, De), lambda i, be, bf: (be[i], 0, 0)),
                  pl.BlockSpec((1, D, De), lambda i, be, bf: (be[i], 0, 0)),
                  pl.BlockSpec((1, De, D), lambda i, be, bf: (be[i], 0, 0))],
        out_specs=pl.BlockSpec((EXPERT_ROWS,) + ROW_TILE, lambda i, be, bf: (i, 0, 0)),
        scratch_shapes=[pltpu.VMEM((D, De), BF16), pltpu.VMEM((D, De), BF16), pltpu.VMEM((De, D), BF16)],
    )
    return pl.pallas_call(
        _moe_experts_kernel,
        grid_spec=grid_spec,
        out_shape=jax.ShapeDtypeStruct((n_rows,) + ROW_TILE, F32),
        compiler_params=_cparams(("arbitrary",)),
        name="moe_experts",
    )(blk_exp, blk_first, xs, w_eg, w_eu, w_ed)


def _moe_combine_kernel(dest_ref, w_ref, pre_ref, gt2_ref, gfin_ref, ys_ref, o_ref, buf, sem):
    tc = pre_ref.shape[0]

    def copy(r, kk):
        slot = r * TOP_K + kk
        return pltpu.make_async_copy(ys_ref.at[dest_ref[slot]], buf.at[slot], sem)

    def issue(r, carry):
        for kk in range(TOP_K):
            copy(r, kk).start()
        return carry

    def drain(r, carry):
        for kk in range(TOP_K):
            copy(r, kk).wait()
        return carry

    lax.fori_loop(0, tc, issue, 0)
    lax.fori_loop(0, tc, drain, 0)
    gt2 = gt2_ref[0]
    gfin = gfin_ref[...]
    n_el = ROW_TILE[0] * ROW_TILE[1] * 1.0

    def token(r, carry):
        routed = jnp.zeros(ROW_TILE, F32)
        for kk in range(TOP_K):
            slot = r * TOP_K + kk
            routed = routed + w_ref[slot] * buf[slot]
        hrow = pre_ref[r] + gt2 * routed
        ms = jnp.sum(hrow * hrow) / n_el
        o_ref[r] = hrow * lax.rsqrt(ms + EPS) * gfin
        return carry

    lax.fori_loop(0, tc, token, 0)


def _moe_combine(ys, dest_flat, w_flat, pre3, gt2_3, gfin3, S):
    T = pre3.shape[0]
    tc = 128
    per_b = S // tc
    return pl.pallas_call(
        _moe_combine_kernel,
        grid=(T // tc,),
        in_specs=[pl.BlockSpec((tc * TOP_K,), lambda i: (i,), memory_space=pltpu.SMEM),
                  pl.BlockSpec((tc * TOP_K,), lambda i: (i,), memory_space=pltpu.SMEM),
                  pl.BlockSpec((tc,) + ROW_TILE, lambda i: (i, 0, 0)),
                  pl.BlockSpec((1,) + ROW_TILE, lambda i: (i // per_b, 0, 0)),
                  pl.BlockSpec(ROW_TILE, lambda i: (0, 0)),
                  pl.BlockSpec(memory_space=pl.ANY)],
        out_specs=pl.BlockSpec((tc,) + ROW_TILE, lambda i: (i, 0, 0)),
        out_shape=jax.ShapeDtypeStruct((T,) + ROW_TILE, F32),
        scratch_shapes=[pltpu.VMEM((tc * TOP_K,) + ROW_TILE, F32), pltpu.SemaphoreType.DMA(())],
        compiler_params=_cparams(("arbitrary",)),
        name="moe_combine",
    )(dest_flat, w_flat, pre3, gt2_3, gfin3, ys)


def _layer(h_res, mod, g_attn, w_in, cmp_pos_k, cmp_w1_k, cmp_w2_k, cmp_pos_v, cmp_w1_v, cmp_w2_v,
           w_out, g_ffn, w_router, router_bias, w_eg, w_eu, w_ed, w_sg, w_su, w_sd):
    B, S, D = h_res.shape
    T = B * S
    mod3 = mod.reshape(B, 6, D)
    (qa, kc, vc, ks, vs, kw, vw, gates, qb, kb, vb, kmean) = _in_proj(h_res, mod3, g_attn, w_in)
    kcmp, vcmp = _compress(kc, vc, cmp_pos_k, cmp_w1_k, cmp_w2_k, cmp_pos_v, cmp_w1_v, cmp_w2_v)
    o_cmp, sel = _nsa_cmp(qa, kcmp, vcmp)
    o_slc = _nsa_flash("slc", qa, ks, vs, sel)
    o_win = _nsa_flash("win", qa, kw, vw)
    o_b = _moba(qb, kb, vb, kmean.reshape(B, S // MOBA_BLOCK, 512))
    h2, pre, eidx, wts, hot, counts = _post_attn(h_res, mod3, o_cmp, o_slc, o_win, gates, o_b, w_out, g_ffn,
                                                 w_router, router_bias, w_sg, w_su, w_sd)
    counts = counts.reshape(N_EXPERTS).astype(I32)
    padded = (counts + EXPERT_ROWS - 1) // EXPERT_ROWS * EXPERT_ROWS
    ends = jnp.cumsum(padded)
    starts = ends - padded
    n_blk = -(-(T * TOP_K) // EXPERT_ROWS) + N_EXPERTS
    n_rows = n_blk * EXPERT_ROWS
    blk_row = jnp.arange(n_blk, dtype=I32) * EXPERT_ROWS
    blk_exp = jnp.minimum(jnp.searchsorted(ends, blk_row, side="right"), N_EXPERTS - 1).astype(I32)
    blk_first = jnp.concatenate([jnp.ones((1,), I32), (blk_exp[1:] != blk_exp[:-1]).astype(I32)])
    dest = _moe_pos(hot.reshape(T, N_EXPERTS), eidx.reshape(T, LANES), starts.astype(F32).reshape(1, N_EXPERTS))
    dest_flat = dest[:, :TOP_K].reshape(T * TOP_K)
    w_flat = wts.reshape(T, LANES)[:, :TOP_K].reshape(T * TOP_K)
    xs = _moe_scatter(h2.reshape((T,) + ROW_TILE), dest_flat, n_rows)
    ys = _moe_experts(xs, blk_exp, blk_first, w_eg, w_eu, w_ed)
    gt2_3 = mod3[:, 5, :].reshape((B,) + ROW_TILE)
    return ys, dest_flat, w_flat, pre.reshape((T,) + ROW_TILE), gt2_3


def kernel(x, c, w_ada, b_ada, g_attn, w_in, cmp_pos_k, cmp_w1_k, cmp_w2_k, cmp_pos_v, cmp_w1_v, cmp_w2_v, w_out, g_ffn, w_router, router_bias, w_exp_gate, w_exp_up, w_exp_down, w_sh_gate, w_sh_up, w_sh_down, g_final):
    B, S, D = x.shape
    depth = w_ada.shape[0]
    assert depth == 1, "the final RMSNorm is fused into the single layer's MoE combine"
    assert D == SUBLANES * LANES and S % SEQ_TILE == 0
    l = 0
    mod = _ada_mod(c.astype(F32), w_ada[l], b_ada[l])
    ys, dest_flat, w_flat, pre3, gt2_3 = _layer(
        x.astype(F32), mod, g_attn[l], w_in[l], cmp_pos_k[l], cmp_w1_k[l], cmp_w2_k[l], cmp_pos_v[l], cmp_w1_v[l],
        cmp_w2_v[l], w_out[l], g_ffn[l], w_router[l], router_bias[l], w_exp_gate[l], w_exp_up[l], w_exp_down[l],
        w_sh_gate[l], w_sh_up[l], w_sh_down[l])
    out3 = _moe_combine(ys, dest_flat, w_flat, pre3, gt2_3, g_final.reshape(ROW_TILE).astype(F32), S)
    return out3.reshape(B, S, D).astype(x.dtype)
```

```python
import functools

import jax
import jax.numpy as jnp
import numpy as np
from jax import lax
from jax.experimental import pallas as pl
from jax.experimental.pallas import tpu as pltpu

F32 = jnp.float32
BF16 = jnp.bfloat16
I32 = jnp.int32

HEAD_DIM = 64
HALF = HEAD_DIM // 2
NSA_HEADS = 8
NSA_CMP_LEN = 32
NSA_CMP_STRIDE = 16
NSA_SEL_LEN = 64
NSA_SEL_TOPK = 16
NSA_WINDOW = 512
MOBA_HEADS = 8
MOBA_BLOCK = 256
MOBA_TOPK = 3
N_EXPERTS = 256
TOP_K = 8
N_GROUPS = 8
TOPK_GROUPS = 4
ROUTED_SCALE = 2.5
ROPE_THETA = 10000.0
EPS = 1e-6
NEG = -1e30

LANES = 128
SUBLANES = 8
VMEM_LIMIT = 56 * 1024 * 1024
EXPERT_ROWS = 256
SEQ_TILE = 256
MOE_TILE = 128


def _cparams(sem):
    return pltpu.CompilerParams(dimension_semantics=sem, vmem_limit_bytes=VMEM_LIMIT)


def _dot(a, b):
    return jnp.dot(a.astype(BF16), b.astype(BF16), preferred_element_type=F32)


def _dot_nt(a, b):
    return lax.dot_general(a.astype(BF16), b.astype(BF16), (((1,), (1,)), ((), ())),
                           preferred_element_type=F32)


def _dot_f32(a, b):
    return jnp.dot(a.astype(F32), b.astype(F32), preferred_element_type=F32)


def _dot_nt_f32(a, b):
    return lax.dot_general(a.astype(F32), b.astype(F32), (((1,), (1,)), ((), ())),
                           preferred_element_type=F32)


def _silu(x):
    return x * jax.nn.sigmoid(x)


def _rms(x, g):
    return x * lax.rsqrt(jnp.mean(x * x, axis=-1, keepdims=True) + EPS) * g


def _topk_rows(score, k):
    n = score.shape[0]
    idx = lax.broadcasted_iota(I32, score.shape, 0)
    rank = jnp.zeros(score.shape, I32)
    for i in range(n):
        row = score[i:i + 1, :]
        beats = (row > score) | ((row == score) & (idx > i))
        rank = rank + beats.astype(I32)
    return rank < k


def _ada_kernel(c_ref, w_ref, b_ref, o_ref):
    o_ref[...] = _dot_f32(_silu(c_ref[...]), w_ref[...]) + b_ref[...]


def _ada_mod(c, w, b):
    B, D = c.shape
    N = w.shape[1]
    tn = 1536 if N % 1536 == 0 else N
    return pl.pallas_call(
        _ada_kernel,
        grid=(N // tn,),
        in_specs=[pl.BlockSpec((B, D), lambda j: (0, 0)),
                  pl.BlockSpec((D, tn), lambda j: (0, j)),
                  pl.BlockSpec((1, tn), lambda j: (0, j))],
        out_specs=pl.BlockSpec((B, tn), lambda j: (0, j)),
        out_shape=jax.ShapeDtypeStruct((B, N), F32),
        compiler_params=_cparams(("arbitrary",)),
        name="ada_mod",
    )(c, w, b.reshape(1, N))


_OFF_QA, _OFF_KV, _OFF_G, _OFF_QB, _OFF_KB, _W_COLS = 0, 512, 896, 1024, 1536, 2048
_VT_ROWS = 640


def _swap_halves(t):
    n = t.shape[-1]
    lane = lax.broadcasted_iota(I32, t.shape, t.ndim - 1)
    left = pltpu.roll(t, n - HALF, t.ndim - 1)
    right = pltpu.roll(t, HALF, t.ndim - 1)
    return jnp.where((lane % HEAD_DIM) < HALF, left, right)


def _in_proj_kernel(x_ref, mod_ref, g_ref, w_ref, wvt_ref, cq_ref, sq_ref, ckv_ref, skv_ref,
                    qa_ref, kc_ref, vc_ref, ks_ref, kw_ref, gt_ref, qb_ref, kb_ref, km_ref,
                    vbt_ref, vst_ref, vwt_ref):
    x = x_ref[0]
    sh1 = mod_ref[0, 0:1, :]
    sc1 = mod_ref[0, 1:2, :]
    h = (_rms(x, g_ref[...]) * (1.0 + sc1) + sh1).astype(BF16)
    proj = _dot(h, w_ref[...])
    cq, sq = cq_ref[...], sq_ref[...]
    scale = HEAD_DIM ** -0.5

    def rope(t, c, s):
        return t * c + _swap_halves(t) * s

    qa = rope(proj[:, _OFF_QA:_OFF_QA + 512], cq, sq) * scale
    qa_ref[0] = qa.astype(BF16)
    kv = rope(proj[:, _OFF_KV:_OFF_KV + 384], ckv_ref[...], skv_ref[...])
    for j, ref in ((0, kc_ref), (1, vc_ref), (2, ks_ref), (4, kw_ref)):
        ref[0] = kv[:, HEAD_DIM * j:HEAD_DIM * (j + 1)].astype(BF16)
    gt_ref[0] = jax.nn.sigmoid(proj[:, _OFF_G:_OFF_G + LANES])
    qb = rope(proj[:, _OFF_QB:_OFF_QB + 512], cq, sq) * scale
    qb_ref[0] = qb.astype(BF16)
    kb = rope(proj[:, _OFF_KB:_OFF_KB + 512], cq, sq)
    kb_ref[0] = kb.astype(BF16)
    km_ref[0, 0] = jnp.mean(kb, axis=0, keepdims=True)
    vt = _dot_nt(wvt_ref[...], h)
    vbt_ref[0] = vt[0:512].astype(BF16)
    vst_ref[0] = vt[512:576].astype(BF16)
    vwt_ref[0] = vt[576:640].astype(BF16)


def _rope_tables(S):
    inv = ROPE_THETA ** (-jnp.arange(0, HEAD_DIM, 2, dtype=F32) / HEAD_DIM)
    ang = jnp.arange(S, dtype=F32)[:, None] * inv[None, :]
    cos, sin = jnp.cos(ang), jnp.sin(ang)
    c_head = jnp.concatenate([cos, cos], axis=-1)
    s_head = jnp.concatenate([-sin, sin], axis=-1)
    cq = jnp.tile(c_head, (1, 8))
    sq = jnp.tile(s_head, (1, 8))
    one, zero = jnp.ones_like(c_head), jnp.zeros_like(s_head)
    ckv = jnp.concatenate([c_head, one, c_head, one, c_head, one], axis=-1)
    skv = jnp.concatenate([s_head, zero, s_head, zero, s_head, zero], axis=-1)
    return cq, sq, ckv, skv


def _in_proj(x, mod3, g_attn, w_in):
    B, S, D = x.shape
    tm = SEQ_TILE
    nb = S // tm
    offs = np.cumsum((512, 64, 64, 64, 64, 64, 64, 24, 512, 512))
    cols = jnp.split(w_in, offs.tolist(), axis=1)
    g_w = jnp.pad(cols[7], ((0, 0), (0, LANES - cols[7].shape[1])))
    w = jnp.concatenate([cols[0]] + cols[1:7] + [g_w, cols[8], cols[9]], axis=1).astype(BF16)
    wvt = jnp.concatenate([cols[10], cols[4], cols[6]], axis=1).T.astype(BF16)
    cq, sq, ckv, skv = _rope_tables(S)
    tok = lambda w_: pl.BlockSpec((1, tm, w_), lambda b, i: (b, i, 0))
    tokt = lambda r_: pl.BlockSpec((1, r_, tm), lambda b, i: (b, 0, i))
    tab = lambda w_: pl.BlockSpec((tm, w_), lambda b, i: (i, 0))
    sds = lambda w_, dt: jax.ShapeDtypeStruct((B, S, w_), dt)
    sdst = lambda r_: jax.ShapeDtypeStruct((B, r_, S), BF16)
    return pl.pallas_call(
        _in_proj_kernel,
        grid=(B, nb),
        in_specs=[tok(D),
                  pl.BlockSpec((1, 6, D), lambda b, i: (b, 0, 0)),
                  pl.BlockSpec((1, D), lambda b, i: (0, 0)),
                  pl.BlockSpec((D, _W_COLS), lambda b, i: (0, 0)),
                  pl.BlockSpec((_VT_ROWS, D), lambda b, i: (0, 0)),
                  tab(512), tab(512), tab(384), tab(384)],
        out_specs=[tok(512)] + [tok(HEAD_DIM)] * 4 + [tok(LANES), tok(512), tok(512),
                   pl.BlockSpec((1, 1, 1, 512), lambda b, i: (b, i, 0, 0)),
                   tokt(512), tokt(HEAD_DIM), tokt(HEAD_DIM)],
        out_shape=[sds(512, BF16)] + [sds(HEAD_DIM, BF16)] * 4 + [sds(LANES, F32), sds(512, BF16), sds(512, BF16),
                   jax.ShapeDtypeStruct((B, nb, 1, 512), F32), sdst(512), sdst(HEAD_DIM), sdst(HEAD_DIM)],
        compiler_params=_cparams(("parallel", "parallel")),
        name="in_proj",
    )(x, mod3, g_attn.reshape(1, D), w, wvt, cq, sq, ckv, skv)


def _compress_kernel(kr_ref, vr_ref, pk_ref, w1k_ref, w2k_ref, pv_ref, w1v_ref, w2vt_ref, ko_ref, vo_ref):
    def hidden(r_ref, p_ref, w1_ref):
        r = r_ref[0]
        half = r.shape[1]
        w1 = w1_ref[...]
        top = _dot(r, w1[:half])
        bot = _dot(r, w1[half:])
        bias = _dot_f32(jnp.broadcast_to(p_ref[...], (SUBLANES, 2 * half)), w1)[0:1]
        n = top.shape[0]
        return _silu(top + pltpu.roll(bot, n - 1, 0) + bias)

    ko_ref[0] = _dot(hidden(kr_ref, pk_ref, w1k_ref), w2k_ref[...]).astype(BF16)
    vo_ref[0] = _dot_nt(w2vt_ref[...], hidden(vr_ref, pv_ref, w1v_ref)).astype(BF16)


def _compress(kc, vc, pos_k, w1_k, w2_k, pos_v, w1_v, w2_v):
    B, S, dh = kc.shape
    ng = S // NSA_CMP_STRIDE
    width = NSA_CMP_STRIDE * dh
    kr = kc.reshape(B, ng, width)
    vr = vc.reshape(B, ng, width)
    full = lambda a: pl.BlockSpec(a.shape, lambda b: (0,) * a.ndim)
    pk = pos_k.reshape(1, NSA_CMP_LEN * dh)
    pv = pos_v.reshape(1, NSA_CMP_LEN * dh)
    args = (kr, vr, pk, w1_k, w2_k, pv, w1_v, w2_v.T)
    return pl.pallas_call(
        _compress_kernel,
        grid=(B,),
        in_specs=[pl.BlockSpec((1, ng, width), lambda b: (b, 0, 0))] * 2 + [full(a) for a in args[2:]],
        out_specs=[pl.BlockSpec((1, ng, dh), lambda b: (b, 0, 0)), pl.BlockSpec((1, dh, ng), lambda b: (b, 0, 0))],
        out_shape=[jax.ShapeDtypeStruct((B, ng, dh), BF16), jax.ShapeDtypeStruct((B, dh, ng), BF16)],
        compiler_params=_cparams(("parallel",)),
        name="nsa_compress",
    )(*args)


def _nsa_cmp_kernel(q_ref, kc_ref, vct_ref, o_ref, selt_ref, *, tq, n_sel):
    qi = pl.program_id(1)
    kc = kc_ref[0]
    vct = vct_ref[0]
    ng = kc.shape[0]
    nidx = lax.broadcasted_iota(I32, (ng, tq), 0)
    t = qi * tq + lax.broadcasted_iota(I32, (ng, tq), 1)
    cmask = (nidx * NSA_CMP_STRIDE + NSA_CMP_LEN - 1 <= t) & (nidx < ng - 1)
    psum = jnp.zeros((ng, tq), F32)
    outs = []
    scores = [_dot_nt(kc, q_ref[0, :, HEAD_DIM * h:HEAD_DIM * (h + 1)]) for h in range(NSA_HEADS)]
    for h in range(NSA_HEADS):
        s = jnp.where(cmask, scores[h], NEG)
        m = jnp.max(s, axis=0, keepdims=True)
        e = jnp.where(cmask, jnp.exp(s - m), 0.0)
        d = jnp.sum(e, axis=0, keepdims=True)
        p = e / jnp.where(d > 0.0, d, 1.0)
        outs.append(_dot(vct, p).T)
        psum = psum + p
    o_ref[0] = jnp.concatenate(outs, axis=1)
    ss = lax.broadcasted_iota(I32, (n_sel, ng), 0) * NSA_SEL_LEN
    cs = lax.broadcasted_iota(I32, (n_sel, ng), 1) * NSA_CMP_STRIDE
    overlap_t = ((cs < ss + NSA_SEL_LEN) & (cs + NSA_CMP_LEN > ss)).astype(F32)
    imp = _dot_f32(overlap_t, psum)
    j = lax.broadcasted_iota(I32, (n_sel, tq), 0)
    cur = (qi * tq + lax.broadcasted_iota(I32, (n_sel, tq), 1)) // NSA_SEL_LEN
    valid = j <= cur
    forced = valid & ((j == 0) | (j == cur) | (j == cur - 1))
    score = jnp.where(forced, jnp.inf, jnp.where(valid, imp, -jnp.inf))
    sel = _topk_rows(score, min(NSA_SEL_TOPK, n_sel)) & valid
    selt_ref[0] = sel.astype(F32)


def _nsa_cmp(qa, kcmp, vcmp_t):
    B, S, W = qa.shape
    tq = SEQ_TILE
    ng = kcmp.shape[1]
    n_sel = S // NSA_SEL_LEN
    return pl.pallas_call(
        functools.partial(_nsa_cmp_kernel, tq=tq, n_sel=n_sel),
        grid=(B, S // tq),
        in_specs=[pl.BlockSpec((1, tq, W), lambda b, i: (b, i, 0)),
                  pl.BlockSpec((1, ng, HEAD_DIM), lambda b, i: (b, 0, 0)),
                  pl.BlockSpec((1, HEAD_DIM, ng), lambda b, i: (b, 0, 0))],
        out_specs=[pl.BlockSpec((1, tq, W), lambda b, i: (b, i, 0)),
                   pl.BlockSpec((1, n_sel, tq), lambda b, i: (b, 0, i))],
        out_shape=[jax.ShapeDtypeStruct((B, S, W), F32), jax.ShapeDtypeStruct((B, n_sel, S), F32)],
        compiler_params=_cparams(("parallel", "parallel")),
        name="nsa_cmp_select",
    )(qa, kcmp, vcmp_t)


def _nsa_flash_kernel(*refs, mode, tq):
    if mode == "slc":
        q_ref, k_ref, vt_ref, selt_ref, o_ref, qs_ref, m_ref, l_ref, acc_ref = refs
    else:
        q_ref, k_ref, vt_ref, o_ref, qs_ref, m_ref, l_ref, acc_ref = refs
    H = NSA_HEADS
    tk = tq
    qi = pl.program_id(1)
    for h in range(H):
        qs_ref[h * tq:(h + 1) * tq, :] = q_ref[0, :, HEAD_DIM * h:HEAD_DIM * (h + 1)]
    m_ref[...] = jnp.full(m_ref.shape, NEG, F32)
    l_ref[...] = jnp.zeros(l_ref.shape, F32)
    acc_ref[...] = jnp.zeros(acc_ref.shape, F32)
    krow = lax.broadcasted_iota(I32, (tk, tq), 0)
    t_pos = qi * tq + lax.broadcasted_iota(I32, (tk, tq), 1)

    def step(kt):
        k0 = pl.multiple_of(kt * tk, tk)
        k = k_ref[0, pl.ds(k0, tk), :]
        vt = vt_ref[0, :, pl.ds(k0, tk)]
        s_all = _dot_nt(k, qs_ref[...])
        kpos = k0 + krow
        if mode == "slc":
            n_sel = selt_ref.shape[1]
            kblk = (k0 + lax.broadcasted_iota(I32, (tk, n_sel), 0)) // NSA_SEL_LEN
            expand = (kblk == lax.broadcasted_iota(I32, (tk, n_sel), 1)).astype(BF16)
            mask = (_dot(expand, selt_ref[0]) > 0.5) & (kpos <= t_pos)
        else:
            mask = (kpos <= t_pos) & (kpos > t_pos - NSA_WINDOW)
        for h in range(H):
            hs = slice(h * tq, (h + 1) * tq)
            s = jnp.where(mask, s_all[:, hs], NEG)
            m_prev = m_ref[:, hs]
            m_new = jnp.maximum(m_prev, jnp.max(s, axis=0, keepdims=True))
            alpha = jnp.exp(m_prev - m_new)
            p = jnp.exp(s - m_new)
            l_ref[:, hs] = alpha * l_ref[:, hs] + jnp.sum(p, axis=0, keepdims=True)
            acc_ref[:, hs] = alpha * acc_ref[:, hs] + _dot(vt, p)
            m_ref[:, hs] = m_new

    step(qi)
    lo = 0 if mode == "slc" else jnp.maximum(qi - NSA_WINDOW // tk, 0)

    def body(kt, carry):
        step(kt)
        return carry

    lax.fori_loop(lo, qi, body, 0)
    o_t = acc_ref[...] / l_ref[...]
    o_ref[0] = jnp.concatenate([o_t[:, h * tq:(h + 1) * tq].T for h in range(H)], axis=1)


def _nsa_flash(mode, qa, k, v_t, sel_t=None):
    B, S, W = qa.shape
    tq = SEQ_TILE
    H = NSA_HEADS
    in_specs = [pl.BlockSpec((1, tq, W), lambda b, i: (b, i, 0)),
                pl.BlockSpec((1, S, HEAD_DIM), lambda b, i: (b, 0, 0)),
                pl.BlockSpec((1, HEAD_DIM, S), lambda b, i: (b, 0, 0))]
    args = [qa, k, v_t]
    if mode == "slc":
        in_specs.append(pl.BlockSpec((1, sel_t.shape[1], tq), lambda b, i: (b, 0, i)))
        args.append(sel_t)
    return pl.pallas_call(
        functools.partial(_nsa_flash_kernel, mode=mode, tq=tq),
        grid=(B, S // tq),
        in_specs=in_specs,
        out_specs=pl.BlockSpec((1, tq, W), lambda b, i: (b, i, 0)),
        out_shape=jax.ShapeDtypeStruct((B, S, W), F32),
        scratch_shapes=[pltpu.VMEM((H * tq, HEAD_DIM), BF16),
                        pltpu.VMEM((1, H * tq), F32),
                        pltpu.VMEM((1, H * tq), F32),
                        pltpu.VMEM((HEAD_DIM, H * tq), F32)],
        compiler_params=_cparams(("parallel", "parallel")),
        name="nsa_flash_" + mode,
    )(*args)


def _moba_kernel(q_ref, k_ref, vt_ref, km_ref, o_ref, qs_ref, sel_ref, m_ref, l_ref, acc_ref, *, nb):
    tq = MOBA_BLOCK
    H = MOBA_HEADS
    own = pl.program_id(1)
    own0 = pl.multiple_of(own * tq, tq)
    krow = lax.broadcasted_iota(I32, (tq, tq), 0)
    qcol = lax.broadcasted_iota(I32, (tq, tq), 1)
    blk = lax.broadcasted_iota(I32, (nb, tq), 0)
    heads = [slice(HEAD_DIM * h, HEAD_DIM * (h + 1)) for h in range(H)]
    for h, sl in enumerate(heads):
        qs_ref[h] = q_ref[0, :, sl]
    gates = [_dot_nt_f32(km_ref[0, :, sl], qs_ref[h]) for h, sl in enumerate(heads)]
    scores = [_dot_nt(k_ref[0, pl.ds(own0, tq), sl], qs_ref[h]) for h, sl in enumerate(heads)]
    for h, sl in enumerate(heads):
        gate = jnp.where(blk < own, gates[h], -jnp.inf)
        sel_ref[h] = (_topk_rows(gate, min(MOBA_TOPK, nb)) & (blk < own)).astype(F32)
        s = jnp.where(krow <= qcol, scores[h], NEG)
        m = jnp.max(s, axis=0, keepdims=True)
        p = jnp.exp(s - m)
        m_ref[h] = m
        l_ref[h] = jnp.sum(p, axis=0, keepdims=True)
        acc_ref[h] = _dot(vt_ref[0, sl, pl.ds(own0, tq)], p)

    def body(n, carry):
        n0 = pl.multiple_of(n * tq, tq)
        scores = [_dot_nt(k_ref[0, pl.ds(n0, tq), sl], qs_ref[h]) for h, sl in enumerate(heads)]
        for h, sl in enumerate(heads):
            chosen = sel_ref[h, pl.ds(n, 1), :] > 0.5
            s = jnp.where(chosen, scores[h], NEG)
            m_prev = m_ref[h]
            m_new = jnp.maximum(m_prev, jnp.max(s, axis=0, keepdims=True))
            alpha = jnp.exp(m_prev - m_new)
            p = jnp.exp(s - m_new)
            l_ref[h] = alpha * l_ref[h] + jnp.sum(p, axis=0, keepdims=True)
            acc_ref[h] = alpha * acc_ref[h] + _dot(vt_ref[0, sl, pl.ds(n0, tq)], p)
            m_ref[h] = m_new
        return carry

    lax.fori_loop(0, own, body, 0)
    o_ref[0] = jnp.concatenate([(acc_ref[h] / l_ref[h]).T for h in range(H)], axis=1)


def _moba(qb, kb, vb_t, kmean):
    B, S, W = qb.shape
    tq = MOBA_BLOCK
    nb = S // tq
    H = MOBA_HEADS
    return pl.pallas_call(
        functools.partial(_moba_kernel, nb=nb),
        grid=(B, nb),
        in_specs=[pl.BlockSpec((1, tq, W), lambda b, i: (b, i, 0)),
                  pl.BlockSpec((1, S, W), lambda b, i: (b, 0, 0)),
                  pl.BlockSpec((1, W, S), lambda b, i: (b, 0, 0)),
                  pl.BlockSpec((1, nb, W), lambda b, i: (b, 0, 0))],
        out_specs=pl.BlockSpec((1, tq, W), lambda b, i: (b, i, 0)),
        out_shape=jax.ShapeDtypeStruct((B, S, W), F32),
        scratch_shapes=[pltpu.VMEM((H, tq, HEAD_DIM), BF16),
                        pltpu.VMEM((H, nb, tq), F32),
                        pltpu.VMEM((H, 1, tq), F32),
                        pltpu.VMEM((H, 1, tq), F32),
                        pltpu.VMEM((H, HEAD_DIM, tq), F32)],
        compiler_params=_cparams(("parallel", "parallel")),
        name="moba",
    )(qb, kb, vb_t, kmean)


def _post_attn_kernel(x_ref, mod_ref, oc_ref, os_ref, ow_ref, gt_ref, ob_ref, wo_ref, gf_ref, wr_ref, rb_ref,
                      wsg_ref, wsu_ref, wsd_ref, h2_ref, pre_ref, eidx_ref, wts_ref, hot_ref, cnt_ref):
    first = (pl.program_id(0) == 0) & (pl.program_id(1) == 0)
    x = x_ref[0]
    tm = x.shape[0]
    gt1 = mod_ref[0, 2:3, :]
    sh2 = mod_ref[0, 3:4, :]
    sc2 = mod_ref[0, 4:5, :]
    gt2 = mod_ref[0, 5:6, :]
    gates = gt_ref[0]
    gi = lax.broadcasted_iota(I32, (LANES, 512), 0)
    hi = lax.broadcasted_iota(I32, (LANES, 512), 1) // HEAD_DIM
    o_a = jnp.zeros((tm, 512), F32)
    for jbr, ref in enumerate((oc_ref, os_ref, ow_ref)):
        expand = (gi == 3 * hi + jbr).astype(F32)
        o_a = o_a + _dot_f32(gates, expand) * ref[0]
    wo = wo_ref[...]
    attn = _dot(o_a, wo[:512]) + _dot(ob_ref[0], wo[512:])
    h1 = x + gt1 * attn
    h2 = _rms(h1, gf_ref[...]) * (1.0 + sc2) + sh2
    h2_ref[0] = h2
    shared = _dot(_silu(_dot(h2, wsg_ref[...])) * _dot(h2, wsu_ref[...]), wsd_ref[...])
    pre_ref[0] = h1 + gt2 * shared
    scores = jax.nn.sigmoid(_dot(h2, wr_ref[...]))
    biased = scores + rb_ref[...]
    E = N_EXPERTS
    per = E // N_GROUPS
    lane = lax.broadcasted_iota(I32, (tm, E), 1)
    grp = lane // per
    gs = []
    for g in range(N_GROUPS):
        vals = jnp.where(grp == g, biased, -jnp.inf)
        m1 = jnp.max(vals, axis=-1, keepdims=True)
        i1 = jnp.min(jnp.where(vals == m1, lane, E), axis=-1, keepdims=True)
        m2 = jnp.max(jnp.where(lane == i1, -jnp.inf, vals), axis=-1, keepdims=True)
        gs.append(m1 + m2)
    emask = jnp.zeros((tm, E), jnp.bool_)
    for g in range(N_GROUPS):
        rank = jnp.zeros((tm, 1), I32)
        for i in range(N_GROUPS):
            beats = (gs[i] > gs[g]) | ((gs[i] == gs[g]) & (i < g))
            rank = rank + beats.astype(I32)
        emask = emask | ((rank < TOPK_GROUPS) & (grp == g))
    cand = jnp.where(emask, biased, -jnp.inf)
    lane128 = lax.broadcasted_iota(I32, (tm, LANES), 1)
    eidx = jnp.zeros((tm, LANES), I32)
    wsel = jnp.zeros((tm, LANES), F32)
    hot = jnp.zeros((tm, E), F32)
    wsum = jnp.zeros((tm, 1), F32)
    for kk in range(TOP_K):
        m = jnp.max(cand, axis=-1, keepdims=True)
        idx = jnp.min(jnp.where(cand == m, lane, E), axis=-1, keepdims=True)
        pick = lane == idx
        w = jnp.sum(jnp.where(pick, scores, 0.0), axis=-1, keepdims=True)
        cand = jnp.where(pick, -jnp.inf, cand)
        hot = hot + pick.astype(F32)
        eidx = jnp.where(lane128 == kk, idx, eidx)
        wsel = jnp.where(lane128 == kk, w, wsel)
        wsum = wsum + w
    eidx_ref[0] = eidx
    wts_ref[0] = wsel / wsum * ROUTED_SCALE
    hot_ref[0] = hot.astype(BF16)

    @pl.when(first)
    def _():
        cnt_ref[...] = jnp.zeros(cnt_ref.shape, F32)

    cnt_ref[...] += jnp.sum(hot, axis=0, keepdims=True)


def _post_attn(x, mod3, o_cmp, o_slc, o_win, gates, o_b, w_out, g_ffn, w_router, router_bias, w_sg, w_su, w_sd):
    B, S, D = x.shape
    tm = SEQ_TILE
    E = N_EXPERTS
    tok = lambda w_: pl.BlockSpec((1, tm, w_), lambda b, i: (b, i, 0))
    full = lambda a: pl.BlockSpec(a.shape, lambda b, i: (0,) * a.ndim)
    sds = lambda w_, dt: jax.ShapeDtypeStruct((B, S, w_), dt)
    consts = (w_out.astype(BF16), g_ffn.reshape(1, D), w_router.astype(BF16), router_bias.reshape(1, E),
              w_sg.astype(BF16), w_su.astype(BF16), w_sd.astype(BF16))
    return pl.pallas_call(
        _post_attn_kernel,
        grid=(B, S // tm),
        in_specs=[tok(D), pl.BlockSpec((1, 6, D), lambda b, i: (b, 0, 0)),
                  tok(512), tok(512), tok(512), tok(LANES), tok(512)] + [full(a) for a in consts],
        out_specs=[tok(D), tok(D), tok(LANES), tok(LANES), tok(E), pl.BlockSpec((1, E), lambda b, i: (0, 0))],
        out_shape=[sds(D, F32), sds(D, F32), sds(LANES, I32), sds(LANES, F32), sds(E, BF16),
                   jax.ShapeDtypeStruct((1, E), F32)],
        compiler_params=_cparams(("arbitrary", "arbitrary")),
        name="post_attn_router",
    )(x, mod3, o_cmp, o_slc, o_win, gates, o_b, *consts)


def _moe_pos_kernel(hot_ref, eidx_ref, start_ref, dest_ref, carry_ref):
    @pl.when(pl.program_id(0) == 0)
    def _():
        carry_ref[...] = start_ref[...]

    hot = hot_ref[...]
    tp, E = hot.shape
    r = lax.broadcasted_iota(I32, (tp, tp), 0)
    c = lax.broadcasted_iota(I32, (tp, tp), 1)
    lower = (c < r).astype(BF16)
    pos = carry_ref[...] + _dot(lower, hot)
    lane = lax.broadcasted_iota(I32, (tp, E), 1)
    lane128 = lax.broadcasted_iota(I32, (tp, LANES), 1)
    eidx = eidx_ref[...]
    dest = jnp.zeros((tp, LANES), I32)
    for kk in range(TOP_K):
        e = eidx[:, kk:kk + 1]
        d = jnp.sum(jnp.where(lane == e, pos, 0.0), axis=-1, keepdims=True)
        dest = jnp.where(lane128 == kk, d.astype(I32), dest)
    dest_ref[...] = dest
    carry_ref[...] += jnp.sum(hot.astype(F32), axis=0, keepdims=True)


def _moe_pos(hot, eidx, starts):
    T, E = hot.shape
    tp = 512 if T % 512 == 0 else SEQ_TILE
    return pl.pallas_call(
        _moe_pos_kernel,
        grid=(T // tp,),
        in_specs=[pl.BlockSpec((tp, E), lambda i: (i, 0)),
                  pl.BlockSpec((tp, LANES), lambda i: (i, 0)),
                  pl.BlockSpec((1, E), lambda i: (0, 0))],
        out_specs=pl.BlockSpec((tp, LANES), lambda i: (i, 0)),
        out_shape=jax.ShapeDtypeStruct((T, LANES), I32),
        scratch_shapes=[pltpu.VMEM((1, E), F32)],
        compiler_params=_cparams(("arbitrary",)),
        name="moe_pos",
    )(hot, eidx, starts)


def _row(ref, start):
    return ref.at[pl.ds(pl.multiple_of(start, SUBLANES), SUBLANES)]


def _moe_scatter_kernel(dest_ref, h_ref, zero_ref, xs_ref, sem):
    del zero_ref
    ts = h_ref.shape[0] // SUBLANES

    def copy(r, kk):
        return pltpu.make_async_copy(_row(h_ref, r * SUBLANES), _row(xs_ref, dest_ref[r * TOP_K + kk]), sem)

    def issue(r, carry):
        for kk in range(TOP_K):
            copy(r, kk).start()
        return carry

    def drain(r, carry):
        for kk in range(TOP_K):
            copy(r, kk).wait()
        return carry

    lax.fori_loop(0, ts, issue, 0)
    lax.fori_loop(0, ts, drain, 0)


def _moe_scatter(h2d, dest8, n_rows):
    T = h2d.shape[0] // SUBLANES
    ts = MOE_TILE
    zeros = jnp.zeros((n_rows * SUBLANES, LANES), F32)
    return pl.pallas_call(
        _moe_scatter_kernel,
        grid=(T // ts,),
        in_specs=[pl.BlockSpec((ts * TOP_K,), lambda i: (i,), memory_space=pltpu.SMEM),
                  pl.BlockSpec((ts * SUBLANES, LANES), lambda i: (i, 0)),
                  pl.BlockSpec(memory_space=pl.ANY)],
        out_specs=pl.BlockSpec(memory_space=pl.ANY),
        out_shape=jax.ShapeDtypeStruct((n_rows * SUBLANES, LANES), F32),
        scratch_shapes=[pltpu.SemaphoreType.DMA(())],
        input_output_aliases={2: 0},
        compiler_params=_cparams(("arbitrary",)),
        name="moe_scatter",
    )(dest8, h2d, zeros)


def _moe_experts_kernel(be_ref, bf_ref, nu_ref, xs_ref, wg_ref, wu_ref, wd_ref, ys_ref, wg_s, wu_s, wd_s):
    i = pl.program_id(0)

    @pl.when(bf_ref[i] == 1)
    def _():
        wg_s[...] = wg_ref[0].astype(BF16)
        wu_s[...] = wu_ref[0].astype(BF16)
        wd_s[...] = wd_ref[0].astype(BF16)

    @pl.when(i < nu_ref[0])
    def _():
        rows = xs_ref.shape[0] // SUBLANES
        x = jnp.concatenate([xs_ref[pl.ds(s, rows, stride=SUBLANES), :].astype(BF16) for s in range(SUBLANES)],
                            axis=1)
        a = _silu(_dot(x, wg_s[...])) * _dot(x, wu_s[...])
        y = _dot(a, wd_s[...])
        for s in range(SUBLANES):
            ys_ref[pl.ds(s, rows, stride=SUBLANES), :] = y[:, LANES * s:LANES * (s + 1)]


def _moe_experts(xs, blk_exp, blk_first, n_used, w_eg, w_eu, w_ed):
    n_rows = xs.shape[0] // SUBLANES
    E, D, De = w_eg.shape
    n_blk = n_rows // EXPERT_ROWS
    blk = lambda i, be, bf, nu: (jnp.minimum(i, nu[0] - 1), 0)
    wsel = lambda i, be, bf, nu: (be[i], 0, 0)
    grid_spec = pltpu.PrefetchScalarGridSpec(
        num_scalar_prefetch=3,
        grid=(n_blk,),
        in_specs=[pl.BlockSpec((EXPERT_ROWS * SUBLANES, LANES), blk),
                  pl.BlockSpec((1, D, De), wsel),
                  pl.BlockSpec((1, D, De), wsel),
                  pl.BlockSpec((1, De, D), wsel)],
        out_specs=pl.BlockSpec((EXPERT_ROWS * SUBLANES, LANES), blk),
        scratch_shapes=[pltpu.VMEM((D, De), BF16), pltpu.VMEM((D, De), BF16), pltpu.VMEM((De, D), BF16)],
    )
    return pl.pallas_call(
        _moe_experts_kernel,
        grid_spec=grid_spec,
        out_shape=jax.ShapeDtypeStruct((n_rows * SUBLANES, LANES), F32),
        compiler_params=_cparams(("arbitrary",)),
        name="moe_experts",
    )(blk_exp, blk_first, n_used, xs, w_eg, w_eu, w_ed)


def _moe_combine_kernel(dcur_ref, dnxt_ref, w_ref, pre_ref, gt2_ref, gfin_ref, ys_ref, o_ref, buf, hbuf, sem):
    i = pl.program_id(0)
    n = pl.num_programs(0)
    tc = pre_ref.shape[0] // SUBLANES
    slot = i % 2

    def copy(d_ref, sl, j):
        return pltpu.make_async_copy(_row(ys_ref, d_ref[j]), _row(buf.at[sl], j * SUBLANES), sem.at[sl])

    def issue(d_ref, sl):
        def body(r, carry):
            for kk in range(TOP_K):
                copy(d_ref, sl, r * TOP_K + kk).start()
            return carry
        lax.fori_loop(0, tc, body, 0)

    @pl.when(i == 0)
    def _():
        issue(dcur_ref, 0)

    @pl.when(i + 1 < n)
    def _():
        issue(dnxt_ref, 1 - slot)

    def drain(r, carry):
        for kk in range(TOP_K):
            copy(dcur_ref, slot, r * TOP_K + kk).wait()
        return carry

    lax.fori_loop(0, tc, drain, 0)
    cur = buf.at[slot]
    gt2 = gt2_ref[...]
    unroll = 4

    def tokens(g, carry):
        for u in range(unroll):
            r = g * unroll + u
            routed = jnp.zeros((SUBLANES, LANES), F32)
            for kk in range(TOP_K):
                j = r * TOP_K + kk
                routed = routed + w_ref[j] * _row(cur, j * SUBLANES)[...]
            _row(hbuf, r * SUBLANES)[...] = _row(pre_ref, r * SUBLANES)[...] + gt2 * routed
        return carry

    lax.fori_loop(0, tc // unroll, tokens, 0)
    h = hbuf[...].reshape(tc, SUBLANES, LANES)
    ms = jnp.sum(jnp.sum(h * h, axis=2, keepdims=True), axis=1, keepdims=True) / (SUBLANES * LANES)
    o = h * lax.rsqrt(ms + EPS) * gfin_ref[...][None]
    o_ref[...] = o.reshape(tc * SUBLANES, LANES)


def _moe_combine(ys, dest8, w_flat, pre2d, gt2_2d, gfin2d, S):
    T = pre2d.shape[0] // SUBLANES
    tc = MOE_TILE
    per_b = S // tc
    n = T // tc
    return pl.pallas_call(
        _moe_combine_kernel,
        grid=(n,),
        in_specs=[pl.BlockSpec((tc * TOP_K,), lambda i: (i,), memory_space=pltpu.SMEM),
                  pl.BlockSpec((tc * TOP_K,), lambda i: (jnp.minimum(i + 1, n - 1),), memory_space=pltpu.SMEM),
                  pl.BlockSpec((tc * TOP_K,), lambda i: (i,), memory_space=pltpu.SMEM),
                  pl.BlockSpec((tc * SUBLANES, LANES), lambda i: (i, 0)),
                  pl.BlockSpec((SUBLANES, LANES), lambda i: (i // per_b, 0)),
                  pl.BlockSpec((SUBLANES, LANES), lambda i: (0, 0)),
                  pl.BlockSpec(memory_space=pl.ANY)],
        out_specs=pl.BlockSpec((tc * SUBLANES, LANES), lambda i: (i, 0)),
        out_shape=jax.ShapeDtypeStruct((T * SUBLANES, LANES), F32),
        scratch_shapes=[pltpu.VMEM((2, tc * TOP_K * SUBLANES, LANES), F32),
                        pltpu.VMEM((tc * SUBLANES, LANES), F32),
                        pltpu.SemaphoreType.DMA((2,))],
        compiler_params=_cparams(("arbitrary",)),
        name="moe_combine",
    )(dest8, dest8, w_flat, pre2d, gt2_2d, gfin2d, ys)


def _layer(h_res, mod, g_attn, w_in, cmp_pos_k, cmp_w1_k, cmp_w2_k, cmp_pos_v, cmp_w1_v, cmp_w2_v,
           w_out, g_ffn, w_router, router_bias, w_eg, w_eu, w_ed, w_sg, w_su, w_sd):
    B, S, D = h_res.shape
    T = B * S
    mod3 = mod.reshape(B, 6, D)
    (qa, kc, vc, ks, kw, gates, qb, kb, kmean, vb_t, vs_t, vw_t) = _in_proj(h_res, mod3, g_attn, w_in)
    kcmp, vcmp_t = _compress(kc, vc, cmp_pos_k, cmp_w1_k, cmp_w2_k, cmp_pos_v, cmp_w1_v, cmp_w2_v)
    o_cmp, sel_t = _nsa_cmp(qa, kcmp, vcmp_t)
    o_slc = _nsa_flash("slc", qa, ks, vs_t, sel_t)
    o_win = _nsa_flash("win", qa, kw, vw_t)
    o_b = _moba(qb, kb, vb_t, kmean.reshape(B, S // MOBA_BLOCK, 512))
    h2, pre, eidx, wts, hot, counts = _post_attn(h_res, mod3, o_cmp, o_slc, o_win, gates, o_b, w_out, g_ffn,
                                                 w_router, router_bias, w_sg, w_su, w_sd)
    counts = counts.reshape(N_EXPERTS).astype(I32)
    padded = (counts + EXPERT_ROWS - 1) // EXPERT_ROWS * EXPERT_ROWS
    ends = jnp.cumsum(padded)
    starts = ends - padded
    n_blk = -(-(T * TOP_K) // EXPERT_ROWS) + N_EXPERTS
    n_rows = n_blk * EXPERT_ROWS
    blk_row = jnp.arange(n_blk, dtype=I32) * EXPERT_ROWS
    blk_exp = jnp.minimum(jnp.sum((ends[None, :] <= blk_row[:, None]).astype(I32), axis=1), N_EXPERTS - 1)
    blk_first = jnp.concatenate([jnp.ones((1,), I32), (blk_exp[1:] != blk_exp[:-1]).astype(I32)])
    n_used = (ends[-1:] // EXPERT_ROWS).astype(I32)
    dest = _moe_pos(hot.reshape(T, N_EXPERTS), eidx.reshape(T, LANES), starts.astype(F32).reshape(1, N_EXPERTS))
    dest8 = (dest[:, :TOP_K] * SUBLANES).reshape(T * TOP_K)
    w_flat = wts.reshape(T, LANES)[:, :TOP_K].reshape(T * TOP_K)
    xs = _moe_scatter(h2.reshape(T * SUBLANES, LANES), dest8, n_rows)
    ys = _moe_experts(xs, blk_exp, blk_first, n_used, w_eg, w_eu, w_ed)
    gt2_2d = mod3[:, 5, :].reshape(B * SUBLANES, LANES)
    return ys, dest8, w_flat, pre.reshape(T * SUBLANES, LANES), gt2_2d


def kernel(x, c, w_ada, b_ada, g_attn, w_in, cmp_pos_k, cmp_w1_k, cmp_w2_k, cmp_pos_v, cmp_w1_v, cmp_w2_v, w_out, g_ffn, w_router, router_bias, w_exp_gate, w_exp_up, w_exp_down, w_sh_gate, w_sh_up, w_sh_down, g_final):
    B, S, D = x.shape
    depth = w_ada.shape[0]
    assert depth == 1, "the final RMSNorm is fused into the single layer's MoE combine"
    assert D == SUBLANES * LANES and S % SEQ_TILE == 0
    l = 0
    mod = _ada_mod(c.astype(F32), w_ada[l], b_ada[l])
    ys, dest8, w_flat, pre2d, gt2_2d = _layer(
        x.astype(F32), mod, g_attn[l], w_in[l], cmp_pos_k[l], cmp_w1_k[l], cmp_w2_k[l], cmp_pos_v[l], cmp_w1_v[l],
        cmp_w2_v[l], w_out[l], g_ffn[l], w_router[l], router_bias[l], w_exp_gate[l], w_exp_up[l], w_exp_down[l],
        w_sh_gate[l], w_sh_up[l], w_sh_down[l])
    out2d = _moe_combine(ys, dest8, w_flat, pre2d, gt2_2d, g_final.reshape(SUBLANES, LANES).astype(F32), S)
    return out2d.reshape(B, S, D).astype(x.dtype)
```

```python
import functools

import jax
import jax.numpy as jnp
import numpy as np
from jax import lax
from jax.experimental import pallas as pl
from jax.experimental.pallas import tpu as pltpu

F32 = jnp.float32
BF16 = jnp.bfloat16
I32 = jnp.int32
U32 = jnp.uint32

HEAD_DIM = 64
HALF = HEAD_DIM // 2
NSA_HEADS = 8
NSA_CMP_LEN = 32
NSA_CMP_STRIDE = 16
NSA_SEL_LEN = 64
NSA_SEL_TOPK = 16
NSA_WINDOW = 512
MOBA_HEADS = 8
MOBA_BLOCK = 256
MOBA_TOPK = 3
N_EXPERTS = 256
TOP_K = 8
N_GROUPS = 8
TOPK_GROUPS = 4
ROUTED_SCALE = 2.5
ROPE_THETA = 10000.0
EPS = 1e-6
NEG = -1e30

LANES = 128
SUBLANES = 8
PACKED_SUBLANES = SUBLANES // 2
VMEM_LIMIT = 56 * 1024 * 1024
EXPERT_ROWS = 256
SEQ_TILE = 256
MOE_TILE = 128


def _cparams(sem):
    return pltpu.CompilerParams(dimension_semantics=sem, vmem_limit_bytes=VMEM_LIMIT)


def _dot(a, b):
    return jnp.dot(a.astype(BF16), b.astype(BF16), preferred_element_type=F32)


def _dot_nt(a, b):
    return lax.dot_general(a.astype(BF16), b.astype(BF16), (((1,), (1,)), ((), ())),
                           preferred_element_type=F32)


def _dot_f32(a, b):
    return jnp.dot(a.astype(F32), b.astype(F32), preferred_element_type=F32)


def _dot_nt_f32(a, b):
    return lax.dot_general(a.astype(F32), b.astype(F32), (((1,), (1,)), ((), ())),
                           preferred_element_type=F32)


def _silu(x):
    return x * jax.nn.sigmoid(x)


def _rms(x, g):
    return x * lax.rsqrt(jnp.mean(x * x, axis=-1, keepdims=True) + EPS) * g


def _topk_rows(score, k):
    n = score.shape[0]
    idx = lax.broadcasted_iota(I32, score.shape, 0)
    rank = jnp.zeros(score.shape, I32)
    for i in range(n):
        row = score[i:i + 1, :]
        beats = (row > score) | ((row == score) & (idx > i))
        rank = rank + beats.astype(I32)
    return rank < k


def _ada_kernel(c_ref, w_ref, b_ref, o_ref):
    o_ref[...] = _dot_f32(_silu(c_ref[...]), w_ref[...]) + b_ref[...]


def _ada_mod(c, w, b):
    B, D = c.shape
    N = w.shape[1]
    tn = 1536 if N % 1536 == 0 else N
    return pl.pallas_call(
        _ada_kernel,
        grid=(N // tn,),
        in_specs=[pl.BlockSpec((B, D), lambda j: (0, 0)),
                  pl.BlockSpec((D, tn), lambda j: (0, j)),
                  pl.BlockSpec((1, tn), lambda j: (0, j))],
        out_specs=pl.BlockSpec((B, tn), lambda j: (0, j)),
        out_shape=jax.ShapeDtypeStruct((B, N), F32),
        compiler_params=_cparams(("arbitrary",)),
        name="ada_mod",
    )(c, w, b.reshape(1, N))


_OFF_QA, _OFF_KV, _OFF_G, _OFF_QB, _OFF_KB, _W_COLS = 0, 512, 896, 1024, 1536, 2048
_VT_ROWS = 640


def _swap_halves(t):
    n = t.shape[-1]
    lane = lax.broadcasted_iota(I32, t.shape, t.ndim - 1)
    left = pltpu.roll(t, n - HALF, t.ndim - 1)
    right = pltpu.roll(t, HALF, t.ndim - 1)
    return jnp.where((lane % HEAD_DIM) < HALF, left, right)


def _in_proj_kernel(x_ref, mod_ref, g_ref, w_ref, wvt_ref, cq_ref, sq_ref, ckv_ref, skv_ref,
                    qa_ref, kc_ref, vc_ref, ks_ref, kw_ref, gt_ref, qb_ref, kb_ref, km_ref,
                    vbt_ref, vst_ref, vwt_ref):
    x = x_ref[0]
    sh1 = mod_ref[0, 0:1, :]
    sc1 = mod_ref[0, 1:2, :]
    h = (_rms(x, g_ref[...]) * (1.0 + sc1) + sh1).astype(BF16)
    proj = _dot(h, w_ref[...])
    cq, sq = cq_ref[...], sq_ref[...]
    scale = HEAD_DIM ** -0.5

    def rope(t, c, s):
        return t * c + _swap_halves(t) * s

    qa = rope(proj[:, _OFF_QA:_OFF_QA + 512], cq, sq) * scale
    qa_ref[0] = qa.astype(BF16)
    kv = rope(proj[:, _OFF_KV:_OFF_KV + 384], ckv_ref[...], skv_ref[...])
    for j, ref in ((0, kc_ref), (1, vc_ref), (2, ks_ref), (4, kw_ref)):
        ref[0] = kv[:, HEAD_DIM * j:HEAD_DIM * (j + 1)].astype(BF16)
    gt_ref[0] = jax.nn.sigmoid(proj[:, _OFF_G:_OFF_G + LANES])
    qb = rope(proj[:, _OFF_QB:_OFF_QB + 512], cq, sq) * scale
    qb_ref[0] = qb.astype(BF16)
    kb = rope(proj[:, _OFF_KB:_OFF_KB + 512], cq, sq)
    kb_ref[0] = kb.astype(BF16)
    km_ref[0, 0] = jnp.mean(kb, axis=0, keepdims=True)
    vt = _dot_nt(wvt_ref[...], h)
    vbt_ref[0] = vt[0:512].astype(BF16)
    vst_ref[0] = vt[512:576].astype(BF16)
    vwt_ref[0] = vt[576:640].astype(BF16)


def _rope_tables(S):
    inv = ROPE_THETA ** (-jnp.arange(0, HEAD_DIM, 2, dtype=F32) / HEAD_DIM)
    ang = jnp.arange(S, dtype=F32)[:, None] * inv[None, :]
    cos, sin = jnp.cos(ang), jnp.sin(ang)
    c_head = jnp.concatenate([cos, cos], axis=-1)
    s_head = jnp.concatenate([-sin, sin], axis=-1)
    cq = jnp.tile(c_head, (1, 8))
    sq = jnp.tile(s_head, (1, 8))
    one, zero = jnp.ones_like(c_head), jnp.zeros_like(s_head)
    ckv = jnp.concatenate([c_head, one, c_head, one, c_head, one], axis=-1)
    skv = jnp.concatenate([s_head, zero, s_head, zero, s_head, zero], axis=-1)
    return cq, sq, ckv, skv


def _in_proj(x, mod3, g_attn, w_in):
    B, S, D = x.shape
    tm = SEQ_TILE
    nb = S // tm
    offs = np.cumsum((512, 64, 64, 64, 64, 64, 64, 24, 512, 512))
    cols = jnp.split(w_in, offs.tolist(), axis=1)
    g_w = jnp.pad(cols[7], ((0, 0), (0, LANES - cols[7].shape[1])))
    w = jnp.concatenate([cols[0]] + cols[1:7] + [g_w, cols[8], cols[9]], axis=1).astype(BF16)
    wvt = jnp.concatenate([cols[10], cols[4], cols[6]], axis=1).T.astype(BF16)
    cq, sq, ckv, skv = _rope_tables(S)
    tok = lambda w_: pl.BlockSpec((1, tm, w_), lambda b, i: (b, i, 0))
    tokt = lambda r_: pl.BlockSpec((1, r_, tm), lambda b, i: (b, 0, i))
    tab = lambda w_: pl.BlockSpec((tm, w_), lambda b, i: (i, 0))
    sds = lambda w_, dt: jax.ShapeDtypeStruct((B, S, w_), dt)
    sdst = lambda r_: jax.ShapeDtypeStruct((B, r_, S), BF16)
    return pl.pallas_call(
        _in_proj_kernel,
        grid=(B, nb),
        in_specs=[tok(D),
                  pl.BlockSpec((1, 6, D), lambda b, i: (b, 0, 0)),
                  pl.BlockSpec((1, D), lambda b, i: (0, 0)),
                  pl.BlockSpec((D, _W_COLS), lambda b, i: (0, 0)),
                  pl.BlockSpec((_VT_ROWS, D), lambda b, i: (0, 0)),
                  tab(512), tab(512), tab(384), tab(384)],
        out_specs=[tok(512)] + [tok(HEAD_DIM)] * 4 + [tok(LANES), tok(512), tok(512),
                   pl.BlockSpec((1, 1, 1, 512), lambda b, i: (b, i, 0, 0)),
                   tokt(512), tokt(HEAD_DIM), tokt(HEAD_DIM)],
        out_shape=[sds(512, BF16)] + [sds(HEAD_DIM, BF16)] * 4 + [sds(LANES, F32), sds(512, BF16), sds(512, BF16),
                   jax.ShapeDtypeStruct((B, nb, 1, 512), F32), sdst(512), sdst(HEAD_DIM), sdst(HEAD_DIM)],
        compiler_params=_cparams(("parallel", "parallel")),
        name="in_proj",
    )(x, mod3, g_attn.reshape(1, D), w, wvt, cq, sq, ckv, skv)


def _compress_kernel(kr_ref, vr_ref, pk_ref, w1k_ref, w2k_ref, pv_ref, w1v_ref, w2vt_ref, ko_ref, vo_ref):
    def hidden(r_ref, p_ref, w1_ref):
        r = r_ref[0]
        half = r.shape[1]
        w1 = w1_ref[...]
        top = _dot(r, w1[:half])
        bot = _dot(r, w1[half:])
        bias = _dot_f32(jnp.broadcast_to(p_ref[...], (SUBLANES, 2 * half)), w1)[0:1]
        n = top.shape[0]
        return _silu(top + pltpu.roll(bot, n - 1, 0) + bias)

    ko_ref[0] = _dot(hidden(kr_ref, pk_ref, w1k_ref), w2k_ref[...]).astype(BF16)
    vo_ref[0] = _dot_nt(w2vt_ref[...], hidden(vr_ref, pv_ref, w1v_ref)).astype(BF16)


def _compress(kc, vc, pos_k, w1_k, w2_k, pos_v, w1_v, w2_v):
    B, S, dh = kc.shape
    ng = S // NSA_CMP_STRIDE
    width = NSA_CMP_STRIDE * dh
    kr = kc.reshape(B, ng, width)
    vr = vc.reshape(B, ng, width)
    full = lambda a: pl.BlockSpec(a.shape, lambda b: (0,) * a.ndim)
    pk = pos_k.reshape(1, NSA_CMP_LEN * dh)
    pv = pos_v.reshape(1, NSA_CMP_LEN * dh)
    args = (kr, vr, pk, w1_k, w2_k, pv, w1_v, w2_v.T)
    return pl.pallas_call(
        _compress_kernel,
        grid=(B,),
        in_specs=[pl.BlockSpec((1, ng, width), lambda b: (b, 0, 0))] * 2 + [full(a) for a in args[2:]],
        out_specs=[pl.BlockSpec((1, ng, dh), lambda b: (b, 0, 0)), pl.BlockSpec((1, dh, ng), lambda b: (b, 0, 0))],
        out_shape=[jax.ShapeDtypeStruct((B, ng, dh), BF16), jax.ShapeDtypeStruct((B, dh, ng), BF16)],
        compiler_params=_cparams(("parallel",)),
        name="nsa_compress",
    )(*args)


def _nsa_cmp_kernel(q_ref, kc_ref, vct_ref, o_ref, selt_ref, *, tq, n_sel):
    qi = pl.program_id(1)
    kc = kc_ref[0]
    vct = vct_ref[0]
    ng = kc.shape[0]
    nidx = lax.broadcasted_iota(I32, (ng, tq), 0)
    t = qi * tq + lax.broadcasted_iota(I32, (ng, tq), 1)
    cmask = (nidx * NSA_CMP_STRIDE + NSA_CMP_LEN - 1 <= t) & (nidx < ng - 1)
    psum = jnp.zeros((ng, tq), F32)
    outs = []
    scores = [_dot_nt(kc, q_ref[0, :, HEAD_DIM * h:HEAD_DIM * (h + 1)]) for h in range(NSA_HEADS)]
    for h in range(NSA_HEADS):
        s = jnp.where(cmask, scores[h], NEG)
        m = jnp.max(s, axis=0, keepdims=True)
        e = jnp.where(cmask, jnp.exp(s - m), 0.0)
        d = jnp.sum(e, axis=0, keepdims=True)
        p = e / jnp.where(d > 0.0, d, 1.0)
        outs.append(_dot(vct, p).T)
        psum = psum + p
    o_ref[0] = jnp.concatenate(outs, axis=1)
    ss = lax.broadcasted_iota(I32, (n_sel, ng), 0) * NSA_SEL_LEN
    cs = lax.broadcasted_iota(I32, (n_sel, ng), 1) * NSA_CMP_STRIDE
    overlap_t = ((cs < ss + NSA_SEL_LEN) & (cs + NSA_CMP_LEN > ss)).astype(F32)
    imp = _dot_f32(overlap_t, psum)
    j = lax.broadcasted_iota(I32, (n_sel, tq), 0)
    cur = (qi * tq + lax.broadcasted_iota(I32, (n_sel, tq), 1)) // NSA_SEL_LEN
    valid = j <= cur
    forced = valid & ((j == 0) | (j == cur) | (j == cur - 1))
    score = jnp.where(forced, jnp.inf, jnp.where(valid, imp, -jnp.inf))
    sel = _topk_rows(score, min(NSA_SEL_TOPK, n_sel)) & valid
    selt_ref[0] = sel.astype(F32)


def _nsa_cmp(qa, kcmp, vcmp_t):
    B, S, W = qa.shape
    tq = SEQ_TILE
    ng = kcmp.shape[1]
    n_sel = S // NSA_SEL_LEN
    return pl.pallas_call(
        functools.partial(_nsa_cmp_kernel, tq=tq, n_sel=n_sel),
        grid=(B, S // tq),
        in_specs=[pl.BlockSpec((1, tq, W), lambda b, i: (b, i, 0)),
                  pl.BlockSpec((1, ng, HEAD_DIM), lambda b, i: (b, 0, 0)),
                  pl.BlockSpec((1, HEAD_DIM, ng), lambda b, i: (b, 0, 0))],
        out_specs=[pl.BlockSpec((1, tq, W), lambda b, i: (b, i, 0)),
                   pl.BlockSpec((1, n_sel, tq), lambda b, i: (b, 0, i))],
        out_shape=[jax.ShapeDtypeStruct((B, S, W), F32), jax.ShapeDtypeStruct((B, n_sel, S), F32)],
        compiler_params=_cparams(("parallel", "parallel")),
        name="nsa_cmp_select",
    )(qa, kcmp, vcmp_t)


def _nsa_flash_kernel(*refs, mode, tq):
    if mode == "slc":
        q_ref, k_ref, vt_ref, selt_ref, o_ref, qs_ref, m_ref, l_ref, acc_ref = refs
    else:
        q_ref, k_ref, vt_ref, o_ref, qs_ref, m_ref, l_ref, acc_ref = refs
    H = NSA_HEADS
    tk = tq
    qi = pl.program_id(1)
    for h in range(H):
        qs_ref[h * tq:(h + 1) * tq, :] = q_ref[0, :, HEAD_DIM * h:HEAD_DIM * (h + 1)]
    m_ref[...] = jnp.full(m_ref.shape, NEG, F32)
    l_ref[...] = jnp.zeros(l_ref.shape, F32)
    acc_ref[...] = jnp.zeros(acc_ref.shape, F32)
    krow = lax.broadcasted_iota(I32, (tk, tq), 0)
    t_pos = qi * tq + lax.broadcasted_iota(I32, (tk, tq), 1)

    def step(kt):
        k0 = pl.multiple_of(kt * tk, tk)
        k = k_ref[0, pl.ds(k0, tk), :]
        vt = vt_ref[0, :, pl.ds(k0, tk)]
        s_all = _dot_nt(k, qs_ref[...])
        kpos = k0 + krow
        if mode == "slc":
            n_sel = selt_ref.shape[1]
            kblk = (k0 + lax.broadcasted_iota(I32, (tk, n_sel), 0)) // NSA_SEL_LEN
            expand = (kblk == lax.broadcasted_iota(I32, (tk, n_sel), 1)).astype(BF16)
            mask = (_dot(expand, selt_ref[0]) > 0.5) & (kpos <= t_pos)
        else:
            mask = (kpos <= t_pos) & (kpos > t_pos - NSA_WINDOW)
        for h in range(H):
            hs = slice(h * tq, (h + 1) * tq)
            s = jnp.where(mask, s_all[:, hs], NEG)
            m_prev = m_ref[:, hs]
            m_new = jnp.maximum(m_prev, jnp.max(s, axis=0, keepdims=True))
            alpha = jnp.exp(m_prev - m_new)
            p = jnp.exp(s - m_new)
            l_ref[:, hs] = alpha * l_ref[:, hs] + jnp.sum(p, axis=0, keepdims=True)
            acc_ref[:, hs] = alpha * acc_ref[:, hs] + _dot(vt, p)
            m_ref[:, hs] = m_new

    step(qi)
    lo = 0 if mode == "slc" else jnp.maximum(qi - NSA_WINDOW // tk, 0)

    def body(kt, carry):
        step(kt)
        return carry

    lax.fori_loop(lo, qi, body, 0)
    o_t = acc_ref[...] / l_ref[...]
    o_ref[0] = jnp.concatenate([o_t[:, h * tq:(h + 1) * tq].T for h in range(H)], axis=1)


def _nsa_flash(mode, qa, k, v_t, sel_t=None):
    B, S, W = qa.shape
    tq = SEQ_TILE
    H = NSA_HEADS
    in_specs = [pl.BlockSpec((1, tq, W), lambda b, i: (b, i, 0)),
                pl.BlockSpec((1, S, HEAD_DIM), lambda b, i: (b, 0, 0)),
                pl.BlockSpec((1, HEAD_DIM, S), lambda b, i: (b, 0, 0))]
    args = [qa, k, v_t]
    if mode == "slc":
        in_specs.append(pl.BlockSpec((1, sel_t.shape[1], tq), lambda b, i: (b, 0, i)))
        args.append(sel_t)
    return pl.pallas_call(
        functools.partial(_nsa_flash_kernel, mode=mode, tq=tq),
        grid=(B, S // tq),
        in_specs=in_specs,
        out_specs=pl.BlockSpec((1, tq, W), lambda b, i: (b, i, 0)),
        out_shape=jax.ShapeDtypeStruct((B, S, W), F32),
        scratch_shapes=[pltpu.VMEM((H * tq, HEAD_DIM), BF16),
                        pltpu.VMEM((1, H * tq), F32),
                        pltpu.VMEM((1, H * tq), F32),
                        pltpu.VMEM((HEAD_DIM, H * tq), F32)],
        compiler_params=_cparams(("parallel", "parallel")),
        name="nsa_flash_" + mode,
    )(*args)


def _moba_kernel(q_ref, k_ref, vt_ref, km_ref, o_ref, qs_ref, sel_ref, m_ref, l_ref, acc_ref, *, nb):
    tq = MOBA_BLOCK
    H = MOBA_HEADS
    own = pl.program_id(1)
    own0 = pl.multiple_of(own * tq, tq)
    krow = lax.broadcasted_iota(I32, (tq, tq), 0)
    qcol = lax.broadcasted_iota(I32, (tq, tq), 1)
    blk = lax.broadcasted_iota(I32, (nb, tq), 0)
    heads = [slice(HEAD_DIM * h, HEAD_DIM * (h + 1)) for h in range(H)]
    for h, sl in enumerate(heads):
        qs_ref[h] = q_ref[0, :, sl]
    gates = [_dot_nt_f32(km_ref[0, :, sl], qs_ref[h]) for h, sl in enumerate(heads)]
    scores = [_dot_nt(k_ref[0, pl.ds(own0, tq), sl], qs_ref[h]) for h, sl in enumerate(heads)]
    for h, sl in enumerate(heads):
        gate = jnp.where(blk < own, gates[h], -jnp.inf)
        sel_ref[h] = (_topk_rows(gate, min(MOBA_TOPK, nb)) & (blk < own)).astype(F32)
        s = jnp.where(krow <= qcol, scores[h], NEG)
        m = jnp.max(s, axis=0, keepdims=True)
        p = jnp.exp(s - m)
        m_ref[h] = m
        l_ref[h] = jnp.sum(p, axis=0, keepdims=True)
        acc_ref[h] = _dot(vt_ref[0, sl, pl.ds(own0, tq)], p)

    def body(n, carry):
        n0 = pl.multiple_of(n * tq, tq)
        scores = [_dot_nt(k_ref[0, pl.ds(n0, tq), sl], qs_ref[h]) for h, sl in enumerate(heads)]
        for h, sl in enumerate(heads):
            chosen = sel_ref[h, pl.ds(n, 1), :] > 0.5
            s = jnp.where(chosen, scores[h], NEG)
            m_prev = m_ref[h]
            m_new = jnp.maximum(m_prev, jnp.max(s, axis=0, keepdims=True))
            alpha = jnp.exp(m_prev - m_new)
            p = jnp.exp(s - m_new)
            l_ref[h] = alpha * l_ref[h] + jnp.sum(p, axis=0, keepdims=True)
            acc_ref[h] = alpha * acc_ref[h] + _dot(vt_ref[0, sl, pl.ds(n0, tq)], p)
            m_ref[h] = m_new
        return carry

    lax.fori_loop(0, own, body, 0)
    o_ref[0] = jnp.concatenate([(acc_ref[h] / l_ref[h]).T for h in range(H)], axis=1)


def _moba(qb, kb, vb_t, kmean):
    B, S, W = qb.shape
    tq = MOBA_BLOCK
    nb = S // tq
    H = MOBA_HEADS
    return pl.pallas_call(
        functools.partial(_moba_kernel, nb=nb),
        grid=(B, nb),
        in_specs=[pl.BlockSpec((1, tq, W), lambda b, i: (b, i, 0)),
                  pl.BlockSpec((1, S, W), lambda b, i: (b, 0, 0)),
                  pl.BlockSpec((1, W, S), lambda b, i: (b, 0, 0)),
                  pl.BlockSpec((1, nb, W), lambda b, i: (b, 0, 0))],
        out_specs=pl.BlockSpec((1, tq, W), lambda b, i: (b, i, 0)),
        out_shape=jax.ShapeDtypeStruct((B, S, W), F32),
        scratch_shapes=[pltpu.VMEM((H, tq, HEAD_DIM), BF16),
                        pltpu.VMEM((H, nb, tq), F32),
                        pltpu.VMEM((H, 1, tq), F32),
                        pltpu.VMEM((H, 1, tq), F32),
                        pltpu.VMEM((H, HEAD_DIM, tq), F32)],
        compiler_params=_cparams(("parallel", "parallel")),
        name="moba",
    )(qb, kb, vb_t, kmean)


def _post_attn_kernel(x_ref, mod_ref, oc_ref, os_ref, ow_ref, gt_ref, ob_ref, wo_ref, gf_ref, wrt_ref, rb_ref,
                      wsg_ref, wsu_ref, wsd_ref, h2_ref, pre_ref, eidx_ref, wts_ref, hot_ref, cnt_ref):
    first = (pl.program_id(0) == 0) & (pl.program_id(1) == 0)
    x = x_ref[0]
    tm = x.shape[0]
    gt1 = mod_ref[0, 2:3, :]
    sh2 = mod_ref[0, 3:4, :]
    sc2 = mod_ref[0, 4:5, :]
    gt2 = mod_ref[0, 5:6, :]
    gates = gt_ref[0]
    gi = lax.broadcasted_iota(I32, (LANES, 512), 0)
    hi = lax.broadcasted_iota(I32, (LANES, 512), 1) // HEAD_DIM
    o_a = jnp.zeros((tm, 512), F32)
    for jbr, ref in enumerate((oc_ref, os_ref, ow_ref)):
        expand = (gi == 3 * hi + jbr).astype(F32)
        o_a = o_a + _dot_f32(gates, expand) * ref[0]
    wo = wo_ref[...]
    attn = _dot(o_a, wo[:512]) + _dot(ob_ref[0], wo[512:])
    h1 = x + gt1 * attn
    h2 = (_rms(h1, gf_ref[...]) * (1.0 + sc2) + sh2).astype(BF16)
    half = h2.shape[1] // 2
    lo = lax.bitcast_convert_type(h2[:, :half].astype(F32), U32) >> 16
    hi = lax.bitcast_convert_type(h2[:, half:].astype(F32), U32) & jnp.uint32(0xFFFF0000)
    words = lo | hi
    for s in range(PACKED_SUBLANES):
        h2_ref[pl.ds(s, tm, stride=PACKED_SUBLANES), :] = words[:, LANES * s:LANES * (s + 1)]
    shared = _dot(_silu(_dot(h2, wsg_ref[...])) * _dot(h2, wsu_ref[...]), wsd_ref[...])
    pre = h1 + gt2 * shared
    for s in range(SUBLANES):
        pre_ref[pl.ds(s, tm, stride=SUBLANES), :] = pre[:, LANES * s:LANES * (s + 1)]
    scores = jax.nn.sigmoid(_dot_nt(wrt_ref[...], h2))
    biased = scores + rb_ref[...]
    E = N_EXPERTS
    per = E // N_GROUPS
    rowp = lax.broadcasted_iota(I32, (per, tm), 0)
    gs, vals = [], []
    for g in range(N_GROUPS):
        v = biased[per * g:per * (g + 1), :]
        m1 = jnp.max(v, axis=0, keepdims=True)
        i1 = jnp.min(jnp.where(v == m1, rowp, per), axis=0, keepdims=True)
        m2 = jnp.max(jnp.where(rowp == i1, -jnp.inf, v), axis=0, keepdims=True)
        gs.append(m1 + m2)
        vals.append(v)
    cands = []
    for g in range(N_GROUPS):
        rank = jnp.zeros((1, tm), I32)
        for i in range(N_GROUPS):
            beats = (gs[i] > gs[g]) | ((gs[i] == gs[g]) & (i < g))
            rank = rank + beats.astype(I32)
        cands.append(jnp.where(rank < TOPK_GROUPS, vals[g], -jnp.inf))
    cand = jnp.concatenate(cands, axis=0)
    row = lax.broadcasted_iota(I32, (E, tm), 0)
    krow = lax.broadcasted_iota(I32, (TOP_K, tm), 0)
    eidx = jnp.zeros((TOP_K, tm), I32)
    wsel = jnp.zeros((TOP_K, tm), F32)
    hot = jnp.zeros((E, tm), F32)
    wsum = jnp.zeros((1, tm), F32)
    for kk in range(TOP_K):
        m = jnp.max(cand, axis=0, keepdims=True)
        idx = jnp.min(jnp.where(cand == m, row, E), axis=0, keepdims=True)
        pick = row == idx
        w = jnp.sum(jnp.where(pick, scores, 0.0), axis=0, keepdims=True)
        cand = jnp.where(pick, -jnp.inf, cand)
        hot = hot + pick.astype(F32)
        eidx = jnp.where(krow == kk, idx, eidx)
        wsel = jnp.where(krow == kk, w, wsel)
        wsum = wsum + w
    eidx_ref[...] = eidx
    wts_ref[...] = wsel / wsum * ROUTED_SCALE
    hot_ref[...] = hot.astype(BF16)

    @pl.when(first)
    def _():
        cnt_ref[...] = jnp.zeros(cnt_ref.shape, F32)

    cnt_ref[...] += jnp.sum(hot, axis=1, keepdims=True)


def _post_attn(x, mod3, o_cmp, o_slc, o_win, gates, o_b, w_out, g_ffn, w_router, router_bias, w_sg, w_su, w_sd):
    B, S, D = x.shape
    tm = SEQ_TILE
    nt = S // tm
    T = B * S
    E = N_EXPERTS
    tok = lambda w_: pl.BlockSpec((1, tm, w_), lambda b, i: (b, i, 0))
    full = lambda a: pl.BlockSpec(a.shape, lambda b, i: (0,) * a.ndim)
    rows = lambda r_: pl.BlockSpec((tm * r_, LANES), lambda b, i: (b * nt + i, 0))
    cols = lambda r_: pl.BlockSpec((r_, tm), lambda b, i: (0, b * nt + i))
    consts = (w_out.astype(BF16), g_ffn.reshape(1, D), w_router.T.astype(BF16), router_bias.reshape(E, 1),
              w_sg.astype(BF16), w_su.astype(BF16), w_sd.astype(BF16))
    return pl.pallas_call(
        _post_attn_kernel,
        grid=(B, nt),
        in_specs=[tok(D), pl.BlockSpec((1, 6, D), lambda b, i: (b, 0, 0)),
                  tok(512), tok(512), tok(512), tok(LANES), tok(512)] + [full(a) for a in consts],
        out_specs=[rows(PACKED_SUBLANES), rows(SUBLANES), cols(TOP_K), cols(TOP_K), cols(E),
                   pl.BlockSpec((E, 1), lambda b, i: (0, 0))],
        out_shape=[jax.ShapeDtypeStruct((T * PACKED_SUBLANES, LANES), U32),
                   jax.ShapeDtypeStruct((T * SUBLANES, LANES), F32),
                   jax.ShapeDtypeStruct((TOP_K, T), I32), jax.ShapeDtypeStruct((TOP_K, T), F32),
                   jax.ShapeDtypeStruct((E, T), BF16), jax.ShapeDtypeStruct((E, 1), F32)],
        compiler_params=_cparams(("arbitrary", "arbitrary")),
        name="post_attn_router",
    )(x, mod3, o_cmp, o_slc, o_win, gates, o_b, *consts)


def _moe_pos_kernel(hot_ref, eidx_ref, start_ref, dest_ref, carry_ref):
    @pl.when(pl.program_id(0) == 0)
    def _():
        carry_ref[...] = start_ref[...]

    hot = hot_ref[...]
    E, tp = hot.shape
    r = lax.broadcasted_iota(I32, (tp, tp), 0)
    c = lax.broadcasted_iota(I32, (tp, tp), 1)
    earlier = (r < c).astype(BF16)
    pos = carry_ref[...] + _dot(hot, earlier)
    row = lax.broadcasted_iota(I32, (E, tp), 0)
    krow = lax.broadcasted_iota(I32, (TOP_K, tp), 0)
    eidx = eidx_ref[...]
    dest = jnp.zeros((TOP_K, tp), I32)
    for kk in range(TOP_K):
        d = jnp.sum(jnp.where(row == eidx[kk:kk + 1, :], pos, 0.0), axis=0, keepdims=True)
        dest = jnp.where(krow == kk, d.astype(I32), dest)
    dest_ref[...] = dest
    carry_ref[...] += jnp.sum(hot.astype(F32), axis=1, keepdims=True)


def _moe_pos(hot, eidx, starts):
    E, T = hot.shape
    tp = 512 if T % 512 == 0 else SEQ_TILE
    return pl.pallas_call(
        _moe_pos_kernel,
        grid=(T // tp,),
        in_specs=[pl.BlockSpec((E, tp), lambda i: (0, i)),
                  pl.BlockSpec((TOP_K, tp), lambda i: (0, i)),
                  pl.BlockSpec((E, 1), lambda i: (0, 0))],
        out_specs=pl.BlockSpec((TOP_K, tp), lambda i: (0, i)),
        out_shape=jax.ShapeDtypeStruct((TOP_K, T), I32),
        scratch_shapes=[pltpu.VMEM((E, 1), F32)],
        compiler_params=_cparams(("arbitrary",)),
        name="moe_pos",
    )(hot, eidx, starts)


def _row(ref, start, n=SUBLANES):
    return ref.at[pl.ds(pl.multiple_of(start, n), n)]


def _moe_scatter_kernel(dest_ref, h_ref, zero_ref, xs_ref, sem):
    del zero_ref
    n = PACKED_SUBLANES
    ts = h_ref.shape[0] // n

    def copy(r, kk):
        return pltpu.make_async_copy(_row(h_ref, r * n, n), _row(xs_ref, dest_ref[r * TOP_K + kk], n), sem)

    def issue(r, carry):
        for kk in range(TOP_K):
            copy(r, kk).start(priority=kk % 2)
        return carry

    def drain(r, carry):
        for kk in range(TOP_K):
            copy(r, kk).wait()
        return carry

    lax.fori_loop(0, ts, issue, 0)
    lax.fori_loop(0, ts, drain, 0)


def _moe_scatter(h2p, dest4, n_rows):
    n = PACKED_SUBLANES
    T = h2p.shape[0] // n
    ts = MOE_TILE
    zeros = jnp.zeros((n_rows * n, LANES), U32)
    return pl.pallas_call(
        _moe_scatter_kernel,
        grid=(T // ts,),
        in_specs=[pl.BlockSpec((ts * TOP_K,), lambda i: (i,), memory_space=pltpu.SMEM),
                  pl.BlockSpec((ts * n, LANES), lambda i: (i, 0)),
                  pl.BlockSpec(memory_space=pl.ANY)],
        out_specs=pl.BlockSpec(memory_space=pl.ANY),
        out_shape=jax.ShapeDtypeStruct((n_rows * n, LANES), U32),
        scratch_shapes=[pltpu.SemaphoreType.DMA(())],
        input_output_aliases={2: 0},
        compiler_params=_cparams(("arbitrary",)),
        name="moe_scatter",
    )(dest4, h2p, zeros)


def _moe_experts_kernel(be_ref, bf_ref, nu_ref, xs_ref, wg_ref, wu_ref, wd_ref, ys_ref, wg_s, wu_s, wd_s):
    i = pl.program_id(0)

    @pl.when(bf_ref[i] == 1)
    def _():
        wg_s[...] = wg_ref[0].astype(BF16)
        wu_s[...] = wu_ref[0].astype(BF16)
        wd_s[...] = wd_ref[0].astype(BF16)

    @pl.when(i < nu_ref[0])
    def _():
        n = PACKED_SUBLANES
        rows = xs_ref.shape[0] // n
        words = [xs_ref[pl.ds(s, rows, stride=n), :] for s in range(n)]
        lo = [lax.bitcast_convert_type(w << 16, F32).astype(BF16) for w in words]
        hi = [lax.bitcast_convert_type(w & jnp.uint32(0xFFFF0000), F32).astype(BF16) for w in words]
        x = jnp.concatenate(lo + hi, axis=1)
        a = _silu(_dot(x, wg_s[...])) * _dot(x, wu_s[...])
        y = _dot(a, wd_s[...])
        for s in range(SUBLANES):
            ys_ref[pl.ds(s, rows, stride=SUBLANES), :] = y[:, LANES * s:LANES * (s + 1)]


def _moe_experts(xs, blk_exp, blk_first, n_used, w_eg, w_eu, w_ed):
    n_rows = xs.shape[0] // PACKED_SUBLANES
    E, D, De = w_eg.shape
    n_blk = n_rows // EXPERT_ROWS
    blk = lambda i, be, bf, nu: (jnp.minimum(i, nu[0] - 1), 0)
    wsel = lambda i, be, bf, nu: (be[i], 0, 0)
    grid_spec = pltpu.PrefetchScalarGridSpec(
        num_scalar_prefetch=3,
        grid=(n_blk,),
        in_specs=[pl.BlockSpec((EXPERT_ROWS * PACKED_SUBLANES, LANES), blk),
                  pl.BlockSpec((1, D, De), wsel),
                  pl.BlockSpec((1, D, De), wsel),
                  pl.BlockSpec((1, De, D), wsel)],
        out_specs=pl.BlockSpec((EXPERT_ROWS * SUBLANES, LANES), blk),
        scratch_shapes=[pltpu.VMEM((D, De), BF16), pltpu.VMEM((D, De), BF16), pltpu.VMEM((De, D), BF16)],
    )
    return pl.pallas_call(
        _moe_experts_kernel,
        grid_spec=grid_spec,
        out_shape=jax.ShapeDtypeStruct((n_rows * SUBLANES, LANES), F32),
        compiler_params=_cparams(("arbitrary",)),
        name="moe_experts",
    )(blk_exp, blk_first, n_used, xs, w_eg, w_eu, w_ed)


def _moe_combine_kernel(dcur_ref, dnxt_ref, w_ref, pre_ref, gt2_ref, gfin_ref, ys_ref, o_ref, buf, hbuf, sem):
    i = pl.program_id(0)
    n = pl.num_programs(0)
    tc = pre_ref.shape[0] // SUBLANES
    slot = i % 2

    def copy(d_ref, sl, j):
        return pltpu.make_async_copy(_row(ys_ref, d_ref[j]), _row(buf.at[sl], j * SUBLANES), sem.at[sl])

    def issue(d_ref, sl):
        def body(r, carry):
            for kk in range(TOP_K):
                copy(d_ref, sl, r * TOP_K + kk).start(priority=kk % 2)
            return carry
        lax.fori_loop(0, tc, body, 0)

    @pl.when(i == 0)
    def _():
        issue(dcur_ref, 0)

    @pl.when(i + 1 < n)
    def _():
        issue(dnxt_ref, 1 - slot)

    def drain(r, carry):
        for kk in range(TOP_K):
            copy(dcur_ref, slot, r * TOP_K + kk).wait()
        return carry

    lax.fori_loop(0, tc, drain, 0)
    cur = buf.at[slot]
    gt2 = gt2_ref[...]
    unroll = 4

    def tokens(g, carry):
        for u in range(unroll):
            r = g * unroll + u
            routed = jnp.zeros((SUBLANES, LANES), F32)
            for kk in range(TOP_K):
                j = r * TOP_K + kk
                routed = routed + w_ref[j] * _row(cur, j * SUBLANES)[...]
            _row(hbuf, r * SUBLANES)[...] = _row(pre_ref, r * SUBLANES)[...] + gt2 * routed
        return carry

    lax.fori_loop(0, tc // unroll, tokens, 0)
    h = jnp.concatenate([hbuf[pl.ds(s, tc, stride=SUBLANES), :] for s in range(SUBLANES)], axis=1)
    o_ref[...] = _rms(h, gfin_ref[...])


def _moe_combine(ys, dest8, w_flat, pre2d, gt2_2d, gfin, S):
    T = pre2d.shape[0] // SUBLANES
    tc = MOE_TILE
    per_b = S // tc
    n = T // tc
    return pl.pallas_call(
        _moe_combine_kernel,
        grid=(n,),
        in_specs=[pl.BlockSpec((tc * TOP_K,), lambda i: (i,), memory_space=pltpu.SMEM),
                  pl.BlockSpec((tc * TOP_K,), lambda i: (jnp.minimum(i + 1, n - 1),), memory_space=pltpu.SMEM),
                  pl.BlockSpec((tc * TOP_K,), lambda i: (i,), memory_space=pltpu.SMEM),
                  pl.BlockSpec((tc * SUBLANES, LANES), lambda i: (i, 0)),
                  pl.BlockSpec((SUBLANES, LANES), lambda i: (i // per_b, 0)),
                  pl.BlockSpec((1, SUBLANES * LANES), lambda i: (0, 0)),
                  pl.BlockSpec(memory_space=pl.ANY)],
        out_specs=pl.BlockSpec((tc, SUBLANES * LANES), lambda i: (i, 0)),
        out_shape=jax.ShapeDtypeStruct((T, SUBLANES * LANES), F32),
        scratch_shapes=[pltpu.VMEM((2, tc * TOP_K * SUBLANES, LANES), F32),
                        pltpu.VMEM((tc * SUBLANES, LANES), F32),
                        pltpu.SemaphoreType.DMA((2,))],
        compiler_params=_cparams(("arbitrary",)),
        name="moe_combine",
    )(dest8, dest8, w_flat, pre2d, gt2_2d, gfin, ys)


def _layer(h_res, mod, g_attn, w_in, cmp_pos_k, cmp_w1_k, cmp_w2_k, cmp_pos_v, cmp_w1_v, cmp_w2_v,
           w_out, g_ffn, w_router, router_bias, w_eg, w_eu, w_ed, w_sg, w_su, w_sd):
    B, S, D = h_res.shape
    T = B * S
    mod3 = mod.reshape(B, 6, D)
    (qa, kc, vc, ks, kw, gates, qb, kb, kmean, vb_t, vs_t, vw_t) = _in_proj(h_res, mod3, g_attn, w_in)
    kcmp, vcmp_t = _compress(kc, vc, cmp_pos_k, cmp_w1_k, cmp_w2_k, cmp_pos_v, cmp_w1_v, cmp_w2_v)
    o_cmp, sel_t = _nsa_cmp(qa, kcmp, vcmp_t)
    o_slc = _nsa_flash("slc", qa, ks, vs_t, sel_t)
    o_win = _nsa_flash("win", qa, kw, vw_t)
    o_b = _moba(qb, kb, vb_t, kmean.reshape(B, S // MOBA_BLOCK, 512))
    h2p, pre2d, eidx, wts, hot, counts = _post_attn(h_res, mod3, o_cmp, o_slc, o_win, gates, o_b, w_out, g_ffn,
                                                    w_router, router_bias, w_sg, w_su, w_sd)
    counts = counts.reshape(N_EXPERTS).astype(I32)
    padded = (counts + EXPERT_ROWS - 1) // EXPERT_ROWS * EXPERT_ROWS
    ends = jnp.cumsum(padded)
    starts = ends - padded
    n_blk = -(-(T * TOP_K) // EXPERT_ROWS) + N_EXPERTS
    n_rows = n_blk * EXPERT_ROWS
    blk_row = jnp.arange(n_blk, dtype=I32) * EXPERT_ROWS
    blk_exp = jnp.minimum(jnp.sum((ends[None, :] <= blk_row[:, None]).astype(I32), axis=1), N_EXPERTS - 1)
    blk_first = jnp.concatenate([jnp.ones((1,), I32), (blk_exp[1:] != blk_exp[:-1]).astype(I32)])
    n_used = (ends[-1:] // EXPERT_ROWS).astype(I32)
    dest = _moe_pos(hot, eidx, starts.astype(F32).reshape(N_EXPERTS, 1))
    dest_flat = dest.T.reshape(T * TOP_K)
    w_flat = wts.T.reshape(T * TOP_K)
    xs = _moe_scatter(h2p, dest_flat * PACKED_SUBLANES, n_rows)
    ys = _moe_experts(xs, blk_exp, blk_first, n_used, w_eg, w_eu, w_ed)
    gt2_2d = mod3[:, 5, :].reshape(B * SUBLANES, LANES)
    return ys, dest_flat * SUBLANES, w_flat, pre2d, gt2_2d


def kernel(x, c, w_ada, b_ada, g_attn, w_in, cmp_pos_k, cmp_w1_k, cmp_w2_k, cmp_pos_v, cmp_w1_v, cmp_w2_v, w_out, g_ffn, w_router, router_bias, w_exp_gate, w_exp_up, w_exp_down, w_sh_gate, w_sh_up, w_sh_down, g_final):
    B, S, D = x.shape
    depth = w_ada.shape[0]
    assert depth == 1, "the final RMSNorm is fused into the single layer's MoE combine"
    assert D == SUBLANES * LANES and S % SEQ_TILE == 0
    l = 0
    mod = _ada_mod(c.astype(F32), w_ada[l], b_ada[l])
    ys, dest8, w_flat, pre2d, gt2_2d = _layer(
        x.astype(F32), mod, g_attn[l], w_in[l], cmp_pos_k[l], cmp_w1_k[l], cmp_w2_k[l], cmp_pos_v[l], cmp_w1_v[l],
        cmp_w2_v[l], w_out[l], g_ffn[l], w_router[l], router_bias[l], w_exp_gate[l], w_exp_up[l], w_exp_down[l],
        w_sh_gate[l], w_sh_up[l], w_sh_down[l])
    out = _moe_combine(ys, dest8, w_flat, pre2d, gt2_2d, g_final.reshape(1, D).astype(F32), S)
    return out.reshape(B, S, D).astype(x.dtype)
```

```python
import functools

import jax
import jax.numpy as jnp
import numpy as np
from jax import lax
from jax.experimental import pallas as pl
from jax.experimental.pallas import tpu as pltpu

F32 = jnp.float32
BF16 = jnp.bfloat16
I32 = jnp.int32
U32 = jnp.uint32

HEAD_DIM = 64
HALF = HEAD_DIM // 2
NSA_HEADS = 8
NSA_CMP_LEN = 32
NSA_CMP_STRIDE = 16
NSA_SEL_LEN = 64
NSA_SEL_TOPK = 16
NSA_WINDOW = 512
MOBA_HEADS = 8
MOBA_BLOCK = 256
MOBA_TOPK = 3
N_EXPERTS = 256
TOP_K = 8
N_GROUPS = 8
TOPK_GROUPS = 4
ROUTED_SCALE = 2.5
ROPE_THETA = 10000.0
EPS = 1e-6
NEG = -1e30
LOG2_E = 1.4426950408889634
ONES_ROWS = 16
VT_HEAD = HEAD_DIM + ONES_ROWS

LANES = 128
SUBLANES = 8
PACKED_SUBLANES = SUBLANES // 2
VMEM_LIMIT = 56 * 1024 * 1024
EXPERT_ROWS = 256
SEQ_TILE = 256
MOE_TILE = 128


def _cparams(sem):
    return pltpu.CompilerParams(dimension_semantics=sem, vmem_limit_bytes=VMEM_LIMIT)


def _dot(a, b):
    return jnp.dot(a.astype(BF16), b.astype(BF16), preferred_element_type=F32)


def _dot_nt(a, b):
    return lax.dot_general(a.astype(BF16), b.astype(BF16), (((1,), (1,)), ((), ())),
                           preferred_element_type=F32)


def _dot_f32(a, b):
    return jnp.dot(a.astype(F32), b.astype(F32), preferred_element_type=F32)


def _dot_nt_f32(a, b):
    return lax.dot_general(a.astype(F32), b.astype(F32), (((1,), (1,)), ((), ())),
                           preferred_element_type=F32)


def _silu(x):
    return x * jax.nn.sigmoid(x)


def _rms(x, g):
    return x * lax.rsqrt(jnp.mean(x * x, axis=-1, keepdims=True) + EPS) * g


def _topk_rows(score, k):
    n = score.shape[0]
    idx = lax.broadcasted_iota(I32, score.shape, 0)
    rank = jnp.zeros(score.shape, I32)
    for i in range(n):
        row = score[i:i + 1, :]
        beats = (row > score) | ((row == score) & (idx > i))
        rank = rank + beats.astype(I32)
    return rank < k


def _ada_kernel(c_ref, w_ref, b_ref, o_ref):
    o_ref[...] = _dot_f32(_silu(c_ref[...]), w_ref[...]) + b_ref[...]


def _ada_mod(c, w, b):
    B, D = c.shape
    N = w.shape[1]
    tn = 1536 if N % 1536 == 0 else N
    return pl.pallas_call(
        _ada_kernel,
        grid=(N // tn,),
        in_specs=[pl.BlockSpec((B, D), lambda j: (0, 0)),
                  pl.BlockSpec((D, tn), lambda j: (0, j)),
                  pl.BlockSpec((1, tn), lambda j: (0, j))],
        out_specs=pl.BlockSpec((B, tn), lambda j: (0, j)),
        out_shape=jax.ShapeDtypeStruct((B, N), F32),
        compiler_params=_cparams(("arbitrary",)),
        name="ada_mod",
    )(c, w, b.reshape(1, N))


_OFF_QA, _OFF_KV, _OFF_G, _OFF_QB, _OFF_KB, _W_COLS = 0, 512, 896, 1024, 1536, 2048
_VT_ROWS = 640


def _swap_halves(t):
    n = t.shape[-1]
    lane = lax.broadcasted_iota(I32, t.shape, t.ndim - 1)
    left = pltpu.roll(t, n - HALF, t.ndim - 1)
    right = pltpu.roll(t, HALF, t.ndim - 1)
    return jnp.where((lane % HEAD_DIM) < HALF, left, right)


def _in_proj_kernel(x_ref, mod_ref, g_ref, w_ref, wvt_ref, cq_ref, sq_ref, ckv_ref, skv_ref,
                    qa_ref, kc_ref, vc_ref, ks_ref, kw_ref, gt_ref, qb_ref, kb_ref, km_ref,
                    vbt_ref, vst_ref, vwt_ref):
    x = x_ref[0]
    sh1 = mod_ref[0, 0:1, :]
    sc1 = mod_ref[0, 1:2, :]
    h = (_rms(x, g_ref[...]) * (1.0 + sc1) + sh1).astype(BF16)
    proj = _dot(h, w_ref[...])
    cq, sq = cq_ref[...], sq_ref[...]
    scale = HEAD_DIM ** -0.5 * LOG2_E

    def rope(t, c, s):
        return t * c + _swap_halves(t) * s

    qa = rope(proj[:, _OFF_QA:_OFF_QA + 512], cq, sq) * scale
    qa_ref[0] = qa.astype(BF16)
    kv = rope(proj[:, _OFF_KV:_OFF_KV + 384], ckv_ref[...], skv_ref[...])
    for j, ref in ((0, kc_ref), (1, vc_ref), (2, ks_ref), (4, kw_ref)):
        ref[0] = kv[:, HEAD_DIM * j:HEAD_DIM * (j + 1)].astype(BF16)
    gt_ref[0] = jax.nn.sigmoid(proj[:, _OFF_G:_OFF_G + LANES])
    qb = rope(proj[:, _OFF_QB:_OFF_QB + 512], cq, sq) * scale
    qb_ref[0] = qb.astype(BF16)
    kb = rope(proj[:, _OFF_KB:_OFF_KB + 512], cq, sq)
    kb_ref[0] = kb.astype(BF16)
    km_ref[0, 0] = jnp.mean(kb, axis=0, keepdims=True)
    vt = _dot_nt(wvt_ref[...], h)
    ones = jnp.ones((ONES_ROWS, vt.shape[1]), BF16)
    for h in range(MOBA_HEADS):
        vbt_ref[0, VT_HEAD * h:VT_HEAD * h + HEAD_DIM] = vt[HEAD_DIM * h:HEAD_DIM * (h + 1)].astype(BF16)
        vbt_ref[0, VT_HEAD * h + HEAD_DIM:VT_HEAD * (h + 1)] = ones
    for ref, r0 in ((vst_ref, 512), (vwt_ref, 576)):
        ref[0, 0:HEAD_DIM] = vt[r0:r0 + HEAD_DIM].astype(BF16)
        ref[0, HEAD_DIM:VT_HEAD] = ones


def _rope_tables(S):
    inv = ROPE_THETA ** (-jnp.arange(0, HEAD_DIM, 2, dtype=F32) / HEAD_DIM)
    ang = jnp.arange(S, dtype=F32)[:, None] * inv[None, :]
    cos, sin = jnp.cos(ang), jnp.sin(ang)
    c_head = jnp.concatenate([cos, cos], axis=-1)
    s_head = jnp.concatenate([-sin, sin], axis=-1)
    cq = jnp.tile(c_head, (1, 8))
    sq = jnp.tile(s_head, (1, 8))
    one, zero = jnp.ones_like(c_head), jnp.zeros_like(s_head)
    ckv = jnp.concatenate([c_head, one, c_head, one, c_head, one], axis=-1)
    skv = jnp.concatenate([s_head, zero, s_head, zero, s_head, zero], axis=-1)
    return cq, sq, ckv, skv


def _in_proj(x, mod3, g_attn, w_in):
    B, S, D = x.shape
    tm = SEQ_TILE
    nb = S // tm
    offs = np.cumsum((512, 64, 64, 64, 64, 64, 64, 24, 512, 512))
    cols = jnp.split(w_in, offs.tolist(), axis=1)
    g_w = jnp.pad(cols[7], ((0, 0), (0, LANES - cols[7].shape[1])))
    w = jnp.concatenate([cols[0]] + cols[1:7] + [g_w, cols[8], cols[9]], axis=1).astype(BF16)
    wvt = jnp.concatenate([cols[10], cols[4], cols[6]], axis=1).T.astype(BF16)
    cq, sq, ckv, skv = _rope_tables(S)
    tok = lambda w_: pl.BlockSpec((1, tm, w_), lambda b, i: (b, i, 0))
    tokt = lambda r_: pl.BlockSpec((1, r_, tm), lambda b, i: (b, 0, i))
    tab = lambda w_: pl.BlockSpec((tm, w_), lambda b, i: (i, 0))
    sds = lambda w_, dt: jax.ShapeDtypeStruct((B, S, w_), dt)
    sdst = lambda r_: jax.ShapeDtypeStruct((B, r_, S), BF16)
    return pl.pallas_call(
        _in_proj_kernel,
        grid=(B, nb),
        in_specs=[tok(D),
                  pl.BlockSpec((1, 6, D), lambda b, i: (b, 0, 0)),
                  pl.BlockSpec((1, D), lambda b, i: (0, 0)),
                  pl.BlockSpec((D, _W_COLS), lambda b, i: (0, 0)),
                  pl.BlockSpec((_VT_ROWS, D), lambda b, i: (0, 0)),
                  tab(512), tab(512), tab(384), tab(384)],
        out_specs=[tok(512)] + [tok(HEAD_DIM)] * 4 + [tok(LANES), tok(512), tok(512),
                   pl.BlockSpec((1, 1, 1, 512), lambda b, i: (b, i, 0, 0)),
                   tokt(MOBA_HEADS * VT_HEAD), tokt(VT_HEAD), tokt(VT_HEAD)],
        out_shape=[sds(512, BF16)] + [sds(HEAD_DIM, BF16)] * 4 + [sds(LANES, F32), sds(512, BF16), sds(512, BF16),
                   jax.ShapeDtypeStruct((B, nb, 1, 512), F32), sdst(MOBA_HEADS * VT_HEAD), sdst(VT_HEAD),
                   sdst(VT_HEAD)],
        compiler_params=_cparams(("parallel", "parallel")),
        name="in_proj",
    )(x, mod3, g_attn.reshape(1, D), w, wvt, cq, sq, ckv, skv)


def _compress_kernel(kr_ref, vr_ref, pk_ref, w1k_ref, w2k_ref, pv_ref, w1v_ref, w2vt_ref, ko_ref, vo_ref):
    def hidden(r_ref, p_ref, w1_ref):
        r = r_ref[0]
        half = r.shape[1]
        w1 = w1_ref[...]
        top = _dot(r, w1[:half])
        bot = _dot(r, w1[half:])
        bias = _dot_f32(jnp.broadcast_to(p_ref[...], (SUBLANES, 2 * half)), w1)[0:1]
        n = top.shape[0]
        return _silu(top + pltpu.roll(bot, n - 1, 0) + bias)

    ko_ref[0] = _dot(hidden(kr_ref, pk_ref, w1k_ref), w2k_ref[...]).astype(BF16)
    vo_ref[0] = _dot_nt(w2vt_ref[...], hidden(vr_ref, pv_ref, w1v_ref)).astype(BF16)


def _compress(kc, vc, pos_k, w1_k, w2_k, pos_v, w1_v, w2_v):
    B, S, dh = kc.shape
    ng = S // NSA_CMP_STRIDE
    width = NSA_CMP_STRIDE * dh
    kr = kc.reshape(B, ng, width)
    vr = vc.reshape(B, ng, width)
    full = lambda a: pl.BlockSpec(a.shape, lambda b: (0,) * a.ndim)
    pk = pos_k.reshape(1, NSA_CMP_LEN * dh)
    pv = pos_v.reshape(1, NSA_CMP_LEN * dh)
    args = (kr, vr, pk, w1_k, w2_k, pv, w1_v, w2_v.T)
    return pl.pallas_call(
        _compress_kernel,
        grid=(B,),
        in_specs=[pl.BlockSpec((1, ng, width), lambda b: (b, 0, 0))] * 2 + [full(a) for a in args[2:]],
        out_specs=[pl.BlockSpec((1, ng, dh), lambda b: (b, 0, 0)), pl.BlockSpec((1, dh, ng), lambda b: (b, 0, 0))],
        out_shape=[jax.ShapeDtypeStruct((B, ng, dh), BF16), jax.ShapeDtypeStruct((B, dh, ng), BF16)],
        compiler_params=_cparams(("parallel",)),
        name="nsa_compress",
    )(*args)


def _nsa_cmp_kernel(q_ref, kc_ref, vct_ref, o_ref, selt_ref, *, tq, n_sel):
    qi = pl.program_id(1)
    kc = kc_ref[0]
    vct = vct_ref[0]
    ng = kc.shape[0]
    nidx = lax.broadcasted_iota(I32, (ng, tq), 0)
    t = qi * tq + lax.broadcasted_iota(I32, (ng, tq), 1)
    cmask = (nidx * NSA_CMP_STRIDE + NSA_CMP_LEN - 1 <= t) & (nidx < ng - 1)
    psum = jnp.zeros((ng, tq), F32)
    outs = []
    scores = [_dot_nt(kc, q_ref[0, :, HEAD_DIM * h:HEAD_DIM * (h + 1)]) for h in range(NSA_HEADS)]
    for h in range(NSA_HEADS):
        s = jnp.where(cmask, scores[h], NEG)
        m = jnp.max(s, axis=0, keepdims=True)
        e = jnp.where(cmask, jnp.exp2(s - m), 0.0)
        d = jnp.sum(e, axis=0, keepdims=True)
        p = e / jnp.where(d > 0.0, d, 1.0)
        outs.append(_dot(vct, p).T)
        psum = psum + p
    o_ref[0] = jnp.concatenate(outs, axis=1)
    ss = lax.broadcasted_iota(I32, (n_sel, ng), 0) * NSA_SEL_LEN
    cs = lax.broadcasted_iota(I32, (n_sel, ng), 1) * NSA_CMP_STRIDE
    overlap_t = ((cs < ss + NSA_SEL_LEN) & (cs + NSA_CMP_LEN > ss)).astype(F32)
    imp = _dot_f32(overlap_t, psum)
    j = lax.broadcasted_iota(I32, (n_sel, tq), 0)
    cur = (qi * tq + lax.broadcasted_iota(I32, (n_sel, tq), 1)) // NSA_SEL_LEN
    valid = j <= cur
    forced = valid & ((j == 0) | (j == cur) | (j == cur - 1))
    score = jnp.where(forced, jnp.inf, jnp.where(valid, imp, -jnp.inf))
    sel = _topk_rows(score, min(NSA_SEL_TOPK, n_sel)) & valid
    selt_ref[0] = sel.astype(F32)


def _nsa_cmp(qa, kcmp, vcmp_t):
    B, S, W = qa.shape
    tq = SEQ_TILE
    ng = kcmp.shape[1]
    n_sel = S // NSA_SEL_LEN
    return pl.pallas_call(
        functools.partial(_nsa_cmp_kernel, tq=tq, n_sel=n_sel),
        grid=(B, S // tq),
        in_specs=[pl.BlockSpec((1, tq, W), lambda b, i: (b, i, 0)),
                  pl.BlockSpec((1, ng, HEAD_DIM), lambda b, i: (b, 0, 0)),
                  pl.BlockSpec((1, HEAD_DIM, ng), lambda b, i: (b, 0, 0))],
        out_specs=[pl.BlockSpec((1, tq, W), lambda b, i: (b, i, 0)),
                   pl.BlockSpec((1, n_sel, tq), lambda b, i: (b, 0, i))],
        out_shape=[jax.ShapeDtypeStruct((B, S, W), F32), jax.ShapeDtypeStruct((B, n_sel, S), F32)],
        compiler_params=_cparams(("parallel", "parallel")),
        name="nsa_cmp_select",
    )(qa, kcmp, vcmp_t)


def _nsa_flash_kernel(*refs, mode, tq):
    if mode == "slc":
        q_ref, k_ref, vt_ref, selt_ref, o_ref, qs_ref, m_ref, acc_ref = refs
    else:
        q_ref, k_ref, vt_ref, o_ref, qs_ref, m_ref, acc_ref = refs
    H = NSA_HEADS
    tk = tq
    qi = pl.program_id(1)
    for h in range(H):
        qs_ref[h * tq:(h + 1) * tq, :] = q_ref[0, :, HEAD_DIM * h:HEAD_DIM * (h + 1)]
    m_ref[...] = jnp.full(m_ref.shape, NEG, F32)
    acc_ref[...] = jnp.zeros(acc_ref.shape, F32)
    krow = lax.broadcasted_iota(I32, (tk, tq), 0)
    t_pos = qi * tq + lax.broadcasted_iota(I32, (tk, tq), 1)

    def step(kt):
        k0 = pl.multiple_of(kt * tk, tk)
        k = k_ref[0, pl.ds(k0, tk), :]
        vt = vt_ref[0, :, pl.ds(k0, tk)]
        s_all = _dot_nt(k, qs_ref[...])
        kpos = k0 + krow
        if mode == "slc":
            n_sel = selt_ref.shape[1]
            kblk = (k0 + lax.broadcasted_iota(I32, (tk, n_sel), 0)) // NSA_SEL_LEN
            expand = (kblk == lax.broadcasted_iota(I32, (tk, n_sel), 1)).astype(BF16)
            mask = (_dot(expand, selt_ref[0]) > 0.5) & (kpos <= t_pos)
        else:
            mask = (kpos <= t_pos) & (kpos > t_pos - NSA_WINDOW)
        for h in range(H):
            hs = slice(h * tq, (h + 1) * tq)
            s = jnp.where(mask, s_all[:, hs], NEG)
            m_prev = m_ref[:, hs]
            m_new = jnp.maximum(m_prev, jnp.max(s, axis=0, keepdims=True))
            alpha = jnp.exp2(m_prev - m_new)
            p = jnp.exp2(s - m_new)
            acc_ref[:, hs] = alpha * acc_ref[:, hs] + _dot(vt, p)
            m_ref[:, hs] = m_new

    step(qi)
    lo = 0 if mode == "slc" else jnp.maximum(qi - NSA_WINDOW // tk, 0)

    def body(kt, carry):
        step(kt)
        return carry

    lax.fori_loop(lo, qi, body, 0)
    o_t = acc_ref[0:HEAD_DIM, :] / acc_ref[HEAD_DIM:HEAD_DIM + 1, :]
    o_ref[0] = jnp.concatenate([o_t[:, h * tq:(h + 1) * tq].T for h in range(H)], axis=1)


def _nsa_flash(mode, qa, k, v_t, sel_t=None):
    B, S, W = qa.shape
    tq = SEQ_TILE
    H = NSA_HEADS
    in_specs = [pl.BlockSpec((1, tq, W), lambda b, i: (b, i, 0)),
                pl.BlockSpec((1, S, HEAD_DIM), lambda b, i: (b, 0, 0)),
                pl.BlockSpec((1, VT_HEAD, S), lambda b, i: (b, 0, 0))]
    args = [qa, k, v_t]
    if mode == "slc":
        in_specs.append(pl.BlockSpec((1, sel_t.shape[1], tq), lambda b, i: (b, 0, i)))
        args.append(sel_t)
    return pl.pallas_call(
        functools.partial(_nsa_flash_kernel, mode=mode, tq=tq),
        grid=(B, S // tq),
        in_specs=in_specs,
        out_specs=pl.BlockSpec((1, tq, W), lambda b, i: (b, i, 0)),
        out_shape=jax.ShapeDtypeStruct((B, S, W), F32),
        scratch_shapes=[pltpu.VMEM((H * tq, HEAD_DIM), BF16),
                        pltpu.VMEM((1, H * tq), F32),
                        pltpu.VMEM((VT_HEAD, H * tq), F32)],
        compiler_params=_cparams(("parallel", "parallel")),
        name="nsa_flash_" + mode,
    )(*args)


def _moba_kernel(q_ref, k_ref, vt_ref, km_ref, o_ref, qs_ref, sel_ref, m_ref, acc_ref, *, nb):
    tq = MOBA_BLOCK
    H = MOBA_HEADS
    own = pl.program_id(1)
    own0 = pl.multiple_of(own * tq, tq)
    krow = lax.broadcasted_iota(I32, (tq, tq), 0)
    qcol = lax.broadcasted_iota(I32, (tq, tq), 1)
    blk = lax.broadcasted_iota(I32, (nb, tq), 0)
    heads = [slice(HEAD_DIM * h, HEAD_DIM * (h + 1)) for h in range(H)]
    vheads = [slice(VT_HEAD * h, VT_HEAD * (h + 1)) for h in range(H)]
    for h, sl in enumerate(heads):
        qs_ref[h] = q_ref[0, :, sl]
    gates = [_dot_nt_f32(km_ref[0, :, sl], qs_ref[h]) for h, sl in enumerate(heads)]
    scores = [_dot_nt(k_ref[0, pl.ds(own0, tq), sl], qs_ref[h]) for h, sl in enumerate(heads)]
    for h in range(H):
        gate = jnp.where(blk < own, gates[h], -jnp.inf)
        sel_ref[h] = (_topk_rows(gate, min(MOBA_TOPK, nb)) & (blk < own)).astype(F32)
        s = jnp.where(krow <= qcol, scores[h], NEG)
        m = jnp.max(s, axis=0, keepdims=True)
        m_ref[h] = m
        acc_ref[h] = _dot(vt_ref[0, vheads[h], pl.ds(own0, tq)], jnp.exp2(s - m))

    def body(n, carry):
        n0 = pl.multiple_of(n * tq, tq)
        scores = [_dot_nt(k_ref[0, pl.ds(n0, tq), sl], qs_ref[h]) for h, sl in enumerate(heads)]
        for h in range(H):
            chosen = sel_ref[h, pl.ds(n, 1), :] > 0.5
            s = scores[h]
            m_prev = m_ref[h]
            m_new = jnp.where(chosen, jnp.maximum(m_prev, jnp.max(s, axis=0, keepdims=True)), m_prev)
            alpha = jnp.exp2(m_prev - m_new)
            p = jnp.exp2(s - m_new)
            pv = _dot(vt_ref[0, vheads[h], pl.ds(n0, tq)], p)
            acc_ref[h] = alpha * acc_ref[h] + jnp.where(chosen, pv, 0.0)
            m_ref[h] = m_new
        return carry

    lax.fori_loop(0, own, body, 0)
    o_ref[0] = jnp.concatenate([(acc_ref[h, 0:HEAD_DIM] / acc_ref[h, HEAD_DIM:HEAD_DIM + 1]).T for h in range(H)],
                               axis=1)


def _moba(qb, kb, vb_t, kmean):
    B, S, W = qb.shape
    tq = MOBA_BLOCK
    nb = S // tq
    H = MOBA_HEADS
    return pl.pallas_call(
        functools.partial(_moba_kernel, nb=nb),
        grid=(B, nb),
        in_specs=[pl.BlockSpec((1, tq, W), lambda b, i: (b, i, 0)),
                  pl.BlockSpec((1, S, W), lambda b, i: (b, 0, 0)),
                  pl.BlockSpec((1, H * VT_HEAD, S), lambda b, i: (b, 0, 0)),
                  pl.BlockSpec((1, nb, W), lambda b, i: (b, 0, 0))],
        out_specs=pl.BlockSpec((1, tq, W), lambda b, i: (b, i, 0)),
        out_shape=jax.ShapeDtypeStruct((B, S, W), F32),
        scratch_shapes=[pltpu.VMEM((H, tq, HEAD_DIM), BF16),
                        pltpu.VMEM((H, nb, tq), F32),
                        pltpu.VMEM((H, 1, tq), F32),
                        pltpu.VMEM((H, VT_HEAD, tq), F32)],
        compiler_params=_cparams(("parallel", "parallel")),
        name="moba",
    )(qb, kb, vb_t, kmean)


def _post_attn_kernel(x_ref, mod_ref, oc_ref, os_ref, ow_ref, gt_ref, ob_ref, wo_ref, gf_ref, wrt_ref, rb_ref,
                      wsg_ref, wsu_ref, wsd_ref, h2_ref, pre_ref, eidx_ref, wts_ref, hot_ref, cnt_ref):
    first = (pl.program_id(0) == 0) & (pl.program_id(1) == 0)
    x = x_ref[0]
    tm = x.shape[0]
    gt1 = mod_ref[0, 2:3, :]
    sh2 = mod_ref[0, 3:4, :]
    sc2 = mod_ref[0, 4:5, :]
    gt2 = mod_ref[0, 5:6, :]
    gates = gt_ref[0]
    gi = lax.broadcasted_iota(I32, (LANES, 512), 0)
    hi = lax.broadcasted_iota(I32, (LANES, 512), 1) // HEAD_DIM
    o_a = jnp.zeros((tm, 512), F32)
    for jbr, ref in enumerate((oc_ref, os_ref, ow_ref)):
        expand = (gi == 3 * hi + jbr).astype(F32)
        o_a = o_a + _dot_f32(gates, expand) * ref[0]
    wo = wo_ref[...]
    attn = _dot(o_a, wo[:512]) + _dot(ob_ref[0], wo[512:])
    h1 = x + gt1 * attn
    h2 = (_rms(h1, gf_ref[...]) * (1.0 + sc2) + sh2).astype(BF16)
    half = h2.shape[1] // 2
    lo = lax.bitcast_convert_type(h2[:, :half].astype(F32), U32) >> 16
    hi = lax.bitcast_convert_type(h2[:, half:].astype(F32), U32) & jnp.uint32(0xFFFF0000)
    words = lo | hi
    for s in range(PACKED_SUBLANES):
        h2_ref[pl.ds(s, tm, stride=PACKED_SUBLANES), :] = words[:, LANES * s:LANES * (s + 1)]
    shared = _dot(_silu(_dot(h2, wsg_ref[...])) * _dot(h2, wsu_ref[...]), wsd_ref[...])
    pre = h1 + gt2 * shared
    for s in range(SUBLANES):
        pre_ref[pl.ds(s, tm, stride=SUBLANES), :] = pre[:, LANES * s:LANES * (s + 1)]
    scores = jax.nn.sigmoid(_dot_nt(wrt_ref[...], h2))
    biased = scores + rb_ref[...]
    E = N_EXPERTS
    per = E // N_GROUPS
    rowp = lax.broadcasted_iota(I32, (per, tm), 0)
    gs, vals = [], []
    for g in range(N_GROUPS):
        v = biased[per * g:per * (g + 1), :]
        m1 = jnp.max(v, axis=0, keepdims=True)
        i1 = jnp.min(jnp.where(v == m1, rowp, per), axis=0, keepdims=True)
        m2 = jnp.max(jnp.where(rowp == i1, -jnp.inf, v), axis=0, keepdims=True)
        gs.append(m1 + m2)
        vals.append(v)
    cands = []
    for g in range(N_GROUPS):
        rank = jnp.zeros((1, tm), I32)
        for i in range(N_GROUPS):
            beats = (gs[i] > gs[g]) | ((gs[i] == gs[g]) & (i < g))
            rank = rank + beats.astype(I32)
        cands.append(jnp.where(rank < TOPK_GROUPS, vals[g], -jnp.inf))
    cand = jnp.concatenate(cands, axis=0)
    row = lax.broadcasted_iota(I32, (E, tm), 0)
    krow = lax.broadcasted_iota(I32, (TOP_K, tm), 0)
    eidx = jnp.zeros((TOP_K, tm), I32)
    wsel = jnp.zeros((TOP_K, tm), F32)
    hot = jnp.zeros((E, tm), F32)
    wsum = jnp.zeros((1, tm), F32)
    for kk in range(TOP_K):
        m = jnp.max(cand, axis=0, keepdims=True)
        idx = jnp.min(jnp.where(cand == m, row, E), axis=0, keepdims=True)
        pick = row == idx
        w = jnp.sum(jnp.where(pick, scores, 0.0), axis=0, keepdims=True)
        cand = jnp.where(pick, -jnp.inf, cand)
        hot = hot + pick.astype(F32)
        eidx = jnp.where(krow == kk, idx, eidx)
        wsel = jnp.where(krow == kk, w, wsel)
        wsum = wsum + w
    eidx_ref[...] = eidx
    wts_ref[...] = wsel / wsum * ROUTED_SCALE
    hot_ref[...] = hot.astype(BF16)

    @pl.when(first)
    def _():
        cnt_ref[...] = jnp.zeros(cnt_ref.shape, F32)

    cnt_ref[...] += jnp.sum(hot, axis=1, keepdims=True)


def _post_attn(x, mod3, o_cmp, o_slc, o_win, gates, o_b, w_out, g_ffn, w_router, router_bias, w_sg, w_su, w_sd):
    B, S, D = x.shape
    tm = SEQ_TILE
    nt = S // tm
    T = B * S
    E = N_EXPERTS
    tok = lambda w_: pl.BlockSpec((1, tm, w_), lambda b, i: (b, i, 0))
    full = lambda a: pl.BlockSpec(a.shape, lambda b, i: (0,) * a.ndim)
    rows = lambda r_: pl.BlockSpec((tm * r_, LANES), lambda b, i: (b * nt + i, 0))
    cols = lambda r_: pl.BlockSpec((r_, tm), lambda b, i: (0, b * nt + i))
    consts = (w_out.astype(BF16), g_ffn.reshape(1, D), w_router.T.astype(BF16), router_bias.reshape(E, 1),
              w_sg.astype(BF16), w_su.astype(BF16), w_sd.astype(BF16))
    return pl.pallas_call(
        _post_attn_kernel,
        grid=(B, nt),
        in_specs=[tok(D), pl.BlockSpec((1, 6, D), lambda b, i: (b, 0, 0)),
                  tok(512), tok(512), tok(512), tok(LANES), tok(512)] + [full(a) for a in consts],
        out_specs=[rows(PACKED_SUBLANES), rows(SUBLANES), cols(TOP_K), cols(TOP_K), cols(E),
                   pl.BlockSpec((E, 1), lambda b, i: (0, 0))],
        out_shape=[jax.ShapeDtypeStruct((T * PACKED_SUBLANES, LANES), U32),
                   jax.ShapeDtypeStruct((T * SUBLANES, LANES), F32),
                   jax.ShapeDtypeStruct((TOP_K, T), I32), jax.ShapeDtypeStruct((TOP_K, T), F32),
                   jax.ShapeDtypeStruct((E, T), BF16), jax.ShapeDtypeStruct((E, 1), F32)],
        compiler_params=_cparams(("arbitrary", "arbitrary")),
        name="post_attn_router",
    )(x, mod3, o_cmp, o_slc, o_win, gates, o_b, *consts)


def _moe_pos_kernel(hot_ref, eidx_ref, start_ref, dest_ref, carry_ref):
    @pl.when(pl.program_id(0) == 0)
    def _():
        carry_ref[...] = start_ref[...]

    hot = hot_ref[...]
    E, tp = hot.shape
    r = lax.broadcasted_iota(I32, (tp, tp), 0)
    c = lax.broadcasted_iota(I32, (tp, tp), 1)
    earlier = (r < c).astype(BF16)
    pos = carry_ref[...] + _dot(hot, earlier)
    row = lax.broadcasted_iota(I32, (E, tp), 0)
    krow = lax.broadcasted_iota(I32, (TOP_K, tp), 0)
    eidx = eidx_ref[...]
    dest = jnp.zeros((TOP_K, tp), I32)
    for kk in range(TOP_K):
        d = jnp.sum(jnp.where(row == eidx[kk:kk + 1, :], pos, 0.0), axis=0, keepdims=True)
        dest = jnp.where(krow == kk, d.astype(I32), dest)
    dest_ref[...] = dest
    carry_ref[...] += jnp.sum(hot.astype(F32), axis=1, keepdims=True)


def _moe_pos(hot, eidx, starts):
    E, T = hot.shape
    tp = 512 if T % 512 == 0 else SEQ_TILE
    return pl.pallas_call(
        _moe_pos_kernel,
        grid=(T // tp,),
        in_specs=[pl.BlockSpec((E, tp), lambda i: (0, i)),
                  pl.BlockSpec((TOP_K, tp), lambda i: (0, i)),
                  pl.BlockSpec((E, 1), lambda i: (0, 0))],
        out_specs=pl.BlockSpec((TOP_K, tp), lambda i: (0, i)),
        out_shape=jax.ShapeDtypeStruct((TOP_K, T), I32),
        scratch_shapes=[pltpu.VMEM((E, 1), F32)],
        compiler_params=_cparams(("arbitrary",)),
        name="moe_pos",
    )(hot, eidx, starts)


def _row(ref, start, n=SUBLANES):
    return ref.at[pl.ds(pl.multiple_of(start, n), n)]


def _moe_scatter_kernel(dest_ref, h_ref, zero_ref, xs_ref, sem):
    del zero_ref
    n = PACKED_SUBLANES
    ts = h_ref.shape[0] // n

    def copy(r, kk):
        return pltpu.make_async_copy(_row(h_ref, r * n, n), xs_ref.at[dest_ref[r * TOP_K + kk]], sem)

    def issue(r, carry):
        for kk in range(TOP_K):
            copy(r, kk).start(priority=kk % 2)
        return carry

    def drain(r, carry):
        for kk in range(TOP_K):
            copy(r, kk).wait()
        return carry

    lax.fori_loop(0, ts, issue, 0)
    lax.fori_loop(0, ts, drain, 0)


def _moe_scatter(h2p, dest4, n_rows):
    n = PACKED_SUBLANES
    T = h2p.shape[0] // n
    ts = MOE_TILE
    zeros = jnp.zeros((n_rows, n, LANES), U32)
    return pl.pallas_call(
        _moe_scatter_kernel,
        grid=(T // ts,),
        in_specs=[pl.BlockSpec((ts * TOP_K,), lambda i: (i,), memory_space=pltpu.SMEM),
                  pl.BlockSpec((ts * n, LANES), lambda i: (i, 0)),
                  pl.BlockSpec(memory_space=pl.ANY)],
        out_specs=pl.BlockSpec(memory_space=pl.ANY),
        out_shape=jax.ShapeDtypeStruct((n_rows, n, LANES), U32),
        scratch_shapes=[pltpu.SemaphoreType.DMA(())],
        input_output_aliases={2: 0},
        compiler_params=_cparams(("arbitrary",)),
        name="moe_scatter",
    )(dest4, h2p, zeros)


def _moe_experts_kernel(be_ref, bf_ref, nu_ref, xs_ref, wg_ref, wu_ref, wd_ref, ys_ref,
                        wg_s, wu_s, wd_s, xbuf, ybuf, sem_in, sem_out):
    i = pl.program_id(0)
    n_used = nu_ref[0]
    rows = EXPERT_ROWS
    slot = i % 2

    def in_copies(blk, sl):
        r0 = pl.multiple_of(blk * rows, rows)
        return [pltpu.make_async_copy(xs_ref.at[pl.ds(r0, rows), s], xbuf.at[sl, s], sem_in.at[sl])
                for s in range(PACKED_SUBLANES)]

    def out_copies(blk, sl):
        r0 = pl.multiple_of(blk * rows, rows)
        return [pltpu.make_async_copy(ybuf.at[sl, s], ys_ref.at[pl.ds(r0, rows), s], sem_out.at[sl])
                for s in range(SUBLANES)]

    @pl.when(i == 0)
    def _():
        for cp in in_copies(0, 0):
            cp.start()

    @pl.when(i + 1 < n_used)
    def _():
        for cp in in_copies(i + 1, 1 - slot):
            cp.start()

    @pl.when(bf_ref[i] == 1)
    def _():
        wg_s[...] = wg_ref[0].astype(BF16)
        wu_s[...] = wu_ref[0].astype(BF16)
        wd_s[...] = wd_ref[0].astype(BF16)

    @pl.when(i < n_used)
    def _():
        for cp in in_copies(i, slot):
            cp.wait()

        @pl.when(i >= 2)
        def _():
            for cp in out_copies(i - 2, slot):
                cp.wait()

        words = [xbuf[slot, s] for s in range(PACKED_SUBLANES)]
        lo = [lax.bitcast_convert_type(w << 16, F32).astype(BF16) for w in words]
        hi = [lax.bitcast_convert_type(w & jnp.uint32(0xFFFF0000), F32).astype(BF16) for w in words]
        x = jnp.concatenate(lo + hi, axis=1)
        a = _silu(_dot(x, wg_s[...])) * _dot(x, wu_s[...])
        y = _dot(a, wd_s[...])
        for s in range(SUBLANES):
            ybuf[slot, s] = y[:, LANES * s:LANES * (s + 1)]
        for cp in out_copies(i, slot):
            cp.start()

        @pl.when(i == n_used - 1)
        def _():
            for cp in out_copies(i, slot):
                cp.wait()

            @pl.when(i >= 1)
            def _():
                for cp in out_copies(i - 1, 1 - slot):
                    cp.wait()


def _moe_experts(xs, blk_exp, blk_first, n_used, w_eg, w_eu, w_ed):
    n_rows = xs.shape[0]
    E, D, De = w_eg.shape
    n_blk = n_rows // EXPERT_ROWS
    wsel = lambda i, be, bf, nu: (be[i], 0, 0)
    grid_spec = pltpu.PrefetchScalarGridSpec(
        num_scalar_prefetch=3,
        grid=(n_blk,),
        in_specs=[pl.BlockSpec(memory_space=pl.ANY),
                  pl.BlockSpec((1, D, De), wsel),
                  pl.BlockSpec((1, D, De), wsel),
                  pl.BlockSpec((1, De, D), wsel)],
        out_specs=pl.BlockSpec(memory_space=pl.ANY),
        scratch_shapes=[pltpu.VMEM((D, De), BF16), pltpu.VMEM((D, De), BF16), pltpu.VMEM((De, D), BF16),
                        pltpu.VMEM((2, PACKED_SUBLANES, EXPERT_ROWS, LANES), U32),
                        pltpu.VMEM((2, SUBLANES, EXPERT_ROWS, LANES), F32),
                        pltpu.SemaphoreType.DMA((2,)), pltpu.SemaphoreType.DMA((2,))],
    )
    return pl.pallas_call(
        _moe_experts_kernel,
        grid_spec=grid_spec,
        out_shape=jax.ShapeDtypeStruct((n_rows, SUBLANES, LANES), F32),
        compiler_params=_cparams(("arbitrary",)),
        name="moe_experts",
    )(blk_exp, blk_first, n_used, xs, w_eg, w_eu, w_ed)


def _moe_combine_kernel(dcur_ref, dnxt_ref, w_ref, pre_ref, gt2_ref, gfin_ref, ys_ref, o_ref, buf, hbuf, sem):
    i = pl.program_id(0)
    n = pl.num_programs(0)
    tc = pre_ref.shape[0] // SUBLANES
    slot = i % 2

    def copy(d_ref, sl, j):
        return pltpu.make_async_copy(ys_ref.at[d_ref[j]], _row(buf.at[sl], j * SUBLANES), sem.at[sl])

    def issue(d_ref, sl):
        def body(r, carry):
            for kk in range(TOP_K):
                copy(d_ref, sl, r * TOP_K + kk).start(priority=kk % 2)
            return carry
        lax.fori_loop(0, tc, body, 0)

    @pl.when(i == 0)
    def _():
        issue(dcur_ref, 0)

    @pl.when(i + 1 < n)
    def _():
        issue(dnxt_ref, 1 - slot)

    def drain(r, carry):
        for kk in range(TOP_K):
            copy(dcur_ref, slot, r * TOP_K + kk).wait()
        return carry

    lax.fori_loop(0, tc, drain, 0)
    cur = buf.at[slot]
    gt2 = gt2_ref[...]
    unroll = 4

    def tokens(g, carry):
        for u in range(unroll):
            r = g * unroll + u
            routed = jnp.zeros((SUBLANES, LANES), F32)
            for kk in range(TOP_K):
                j = r * TOP_K + kk
                routed = routed + w_ref[j] * _row(cur, j * SUBLANES)[...]
            _row(hbuf, r * SUBLANES)[...] = _row(pre_ref, r * SUBLANES)[...] + gt2 * routed
        return carry

    lax.fori_loop(0, tc // unroll, tokens, 0)
    h = jnp.concatenate([hbuf[pl.ds(s, tc, stride=SUBLANES), :] for s in range(SUBLANES)], axis=1)
    o_ref[...] = _rms(h, gfin_ref[...])


def _moe_combine(ys, dest8, w_flat, pre2d, gt2_2d, gfin, S):
    T = pre2d.shape[0] // SUBLANES
    tc = MOE_TILE
    per_b = S // tc
    n = T // tc
    return pl.pallas_call(
        _moe_combine_kernel,
        grid=(n,),
        in_specs=[pl.BlockSpec((tc * TOP_K,), lambda i: (i,), memory_space=pltpu.SMEM),
                  pl.BlockSpec((tc * TOP_K,), lambda i: (jnp.minimum(i + 1, n - 1),), memory_space=pltpu.SMEM),
                  pl.BlockSpec((tc * TOP_K,), lambda i: (i,), memory_space=pltpu.SMEM),
                  pl.BlockSpec((tc * SUBLANES, LANES), lambda i: (i, 0)),
                  pl.BlockSpec((SUBLANES, LANES), lambda i: (i // per_b, 0)),
                  pl.BlockSpec((1, SUBLANES * LANES), lambda i: (0, 0)),
                  pl.BlockSpec(memory_space=pl.ANY)],
        out_specs=pl.BlockSpec((tc, SUBLANES * LANES), lambda i: (i, 0)),
        out_shape=jax.ShapeDtypeStruct((T, SUBLANES * LANES), F32),
        scratch_shapes=[pltpu.VMEM((2, tc * TOP_K * SUBLANES, LANES), F32),
                        pltpu.VMEM((tc * SUBLANES, LANES), F32),
                        pltpu.SemaphoreType.DMA((2,))],
        compiler_params=_cparams(("arbitrary",)),
        name="moe_combine",
    )(dest8, dest8, w_flat, pre2d, gt2_2d, gfin, ys)


def _layer(h_res, mod, g_attn, w_in, cmp_pos_k, cmp_w1_k, cmp_w2_k, cmp_pos_v, cmp_w1_v, cmp_w2_v,
           w_out, g_ffn, w_router, router_bias, w_eg, w_eu, w_ed, w_sg, w_su, w_sd):
    B, S, D = h_res.shape
    T = B * S
    mod3 = mod.reshape(B, 6, D)
    (qa, kc, vc, ks, kw, gates, qb, kb, kmean, vb_t, vs_t, vw_t) = _in_proj(h_res, mod3, g_attn, w_in)
    kcmp, vcmp_t = _compress(kc, vc, cmp_pos_k, cmp_w1_k, cmp_w2_k, cmp_pos_v, cmp_w1_v, cmp_w2_v)
    o_cmp, sel_t = _nsa_cmp(qa, kcmp, vcmp_t)
    o_slc = _nsa_flash("slc", qa, ks, vs_t, sel_t)
    o_win = _nsa_flash("win", qa, kw, vw_t)
    o_b = _moba(qb, kb, vb_t, kmean.reshape(B, S // MOBA_BLOCK, 512))
    h2p, pre2d, eidx, wts, hot, counts = _post_attn(h_res, mod3, o_cmp, o_slc, o_win, gates, o_b, w_out, g_ffn,
                                                    w_router, router_bias, w_sg, w_su, w_sd)
    counts = counts.reshape(N_EXPERTS).astype(I32)
    padded = (counts + EXPERT_ROWS - 1) // EXPERT_ROWS * EXPERT_ROWS
    ends = jnp.cumsum(padded)
    starts = ends - padded
    n_blk = -(-(T * TOP_K) // EXPERT_ROWS) + N_EXPERTS
    n_rows = n_blk * EXPERT_ROWS
    blk_row = jnp.arange(n_blk, dtype=I32) * EXPERT_ROWS
    blk_exp = jnp.minimum(jnp.sum((ends[None, :] <= blk_row[:, None]).astype(I32), axis=1), N_EXPERTS - 1)
    blk_first = jnp.concatenate([jnp.ones((1,), I32), (blk_exp[1:] != blk_exp[:-1]).astype(I32)])
    n_used = (ends[-1:] // EXPERT_ROWS).astype(I32)
    dest = _moe_pos(hot, eidx, starts.astype(F32).reshape(N_EXPERTS, 1))
    dest_flat = dest.T.reshape(T * TOP_K)
    w_flat = wts.T.reshape(T * TOP_K)
    xs = _moe_scatter(h2p, dest_flat, n_rows)
    ys = _moe_experts(xs, blk_exp, blk_first, n_used, w_eg, w_eu, w_ed)
    gt2_2d = mod3[:, 5, :].reshape(B * SUBLANES, LANES)
    return ys, dest_flat, w_flat, pre2d, gt2_2d


def kernel(x, c, w_ada, b_ada, g_attn, w_in, cmp_pos_k, cmp_w1_k, cmp_w2_k, cmp_pos_v, cmp_w1_v, cmp_w2_v, w_out, g_ffn, w_router, router_bias, w_exp_gate, w_exp_up, w_exp_down, w_sh_gate, w_sh_up, w_sh_down, g_final):
    B, S, D = x.shape
    depth = w_ada.shape[0]
    assert depth == 1, "the final RMSNorm is fused into the single layer's MoE combine"
    assert D == SUBLANES * LANES and S % SEQ_TILE == 0
    l = 0
    mod = _ada_mod(c.astype(F32), w_ada[l], b_ada[l])
    ys, dest8, w_flat, pre2d, gt2_2d = _layer(
        x.astype(F32), mod, g_attn[l], w_in[l], cmp_pos_k[l], cmp_w1_k[l], cmp_w2_k[l], cmp_pos_v[l], cmp_w1_v[l],
        cmp_w2_v[l], w_out[l], g_ffn[l], w_router[l], router_bias[l], w_exp_gate[l], w_exp_up[l], w_exp_down[l],
        w_sh_gate[l], w_sh_up[l], w_sh_down[l])
    out = _moe_combine(ys, dest8, w_flat, pre2d, gt2_2d, g_final.reshape(1, D).astype(F32), S)
    return out.reshape(B, S, D).astype(x.dtype)
```

```python
import functools

import jax
import jax.numpy as jnp
import numpy as np
from jax import lax
from jax.experimental import pallas as pl
from jax.experimental.pallas import tpu as pltpu

F32 = jnp.float32
BF16 = jnp.bfloat16
I32 = jnp.int32
U32 = jnp.uint32

HEAD_DIM = 64
HALF = HEAD_DIM // 2
NSA_HEADS = 8
NSA_CMP_LEN = 32
NSA_CMP_STRIDE = 16
NSA_SEL_LEN = 64
NSA_SEL_TOPK = 16
NSA_WINDOW = 512
MOBA_HEADS = 8
MOBA_BLOCK = 256
MOBA_TOPK = 3
N_EXPERTS = 256
TOP_K = 8
N_GROUPS = 8
TOPK_GROUPS = 4
ROUTED_SCALE = 2.5
ROPE_THETA = 10000.0
EPS = 1e-6
NEG = -1e30
LOG2_E = 1.4426950408889634
ONES_ROWS = 16
VT_HEAD = HEAD_DIM + ONES_ROWS

LANES = 128
SUBLANES = 8
PACKED_SUBLANES = SUBLANES // 2
VMEM_LIMIT = 56 * 1024 * 1024
EXPERT_ROWS = 256
SEQ_TILE = 256
MOE_TILE = 128
IN_SLOTS = 4


def _cparams(sem):
    return pltpu.CompilerParams(dimension_semantics=sem, vmem_limit_bytes=VMEM_LIMIT)


def _dot(a, b):
    return jnp.dot(a.astype(BF16), b.astype(BF16), preferred_element_type=F32)


def _dot_nt(a, b):
    return lax.dot_general(a.astype(BF16), b.astype(BF16), (((1,), (1,)), ((), ())),
                           preferred_element_type=F32)


def _dot_f32(a, b):
    return jnp.dot(a.astype(F32), b.astype(F32), preferred_element_type=F32)


def _dot_nt_f32(a, b):
    return lax.dot_general(a.astype(F32), b.astype(F32), (((1,), (1,)), ((), ())),
                           preferred_element_type=F32)


def _silu(x):
    return x * jax.nn.sigmoid(x)


def _rms(x, g):
    return x * lax.rsqrt(jnp.mean(x * x, axis=-1, keepdims=True) + EPS) * g


def _topk_rows(score, k):
    n = score.shape[0]
    idx = lax.broadcasted_iota(I32, score.shape, 0)
    rank = jnp.zeros(score.shape, I32)
    for i in range(n):
        row = score[i:i + 1, :]
        beats = (row > score) | ((row == score) & (idx > i))
        rank = rank + beats.astype(I32)
    return rank < k


def _ada_kernel(c_ref, w_ref, b_ref, o_ref):
    o_ref[...] = _dot_f32(_silu(c_ref[...]), w_ref[...]) + b_ref[...]


def _ada_mod(c, w, b):
    B, D = c.shape
    N = w.shape[1]
    tn = 1536 if N % 1536 == 0 else N
    return pl.pallas_call(
        _ada_kernel,
        grid=(N // tn,),
        in_specs=[pl.BlockSpec((B, D), lambda j: (0, 0)),
                  pl.BlockSpec((D, tn), lambda j: (0, j)),
                  pl.BlockSpec((1, tn), lambda j: (0, j))],
        out_specs=pl.BlockSpec((B, tn), lambda j: (0, j)),
        out_shape=jax.ShapeDtypeStruct((B, N), F32),
        compiler_params=_cparams(("arbitrary",)),
        name="ada_mod",
    )(c, w, b.reshape(1, N))


_OFF_QA, _OFF_KV, _OFF_G, _OFF_QB, _OFF_KB, _W_COLS = 0, 512, 896, 1024, 1536, 2048
_VT_ROWS = 640


def _swap_halves(t):
    n = t.shape[-1]
    lane = lax.broadcasted_iota(I32, t.shape, t.ndim - 1)
    left = pltpu.roll(t, n - HALF, t.ndim - 1)
    right = pltpu.roll(t, HALF, t.ndim - 1)
    return jnp.where((lane % HEAD_DIM) < HALF, left, right)


def _in_proj_kernel(x_ref, mod_ref, g_ref, w_ref, wvt_ref, cq_ref, sq_ref, ckv_ref, skv_ref,
                    qa_ref, kc_ref, vc_ref, ks_ref, kw_ref, gt_ref, qb_ref, kb_ref, km_ref,
                    vbt_ref, vst_ref, vwt_ref):
    x = x_ref[0]
    sh1 = mod_ref[0, 0:1, :]
    sc1 = mod_ref[0, 1:2, :]
    h = (_rms(x, g_ref[...]) * (1.0 + sc1) + sh1).astype(BF16)
    proj = _dot(h, w_ref[...])
    cq, sq = cq_ref[...], sq_ref[...]
    scale = HEAD_DIM ** -0.5 * LOG2_E

    def rope(t, c, s):
        return t * c + _swap_halves(t) * s

    qa = rope(proj[:, _OFF_QA:_OFF_QA + 512], cq, sq) * scale
    qa_ref[0] = qa.astype(BF16)
    kv = rope(proj[:, _OFF_KV:_OFF_KV + 384], ckv_ref[...], skv_ref[...])
    for j, ref in ((0, kc_ref), (1, vc_ref), (2, ks_ref), (4, kw_ref)):
        ref[0] = kv[:, HEAD_DIM * j:HEAD_DIM * (j + 1)].astype(BF16)
    gt_ref[0] = jax.nn.sigmoid(proj[:, _OFF_G:_OFF_G + LANES])
    qb = rope(proj[:, _OFF_QB:_OFF_QB + 512], cq, sq) * scale
    qb_ref[0] = qb.astype(BF16)
    kb = rope(proj[:, _OFF_KB:_OFF_KB + 512], cq, sq)
    kb_ref[0] = kb.astype(BF16)
    km_ref[0, 0] = jnp.mean(kb, axis=0, keepdims=True)
    vt = _dot_nt(wvt_ref[...], h)
    ones = jnp.ones((ONES_ROWS, vt.shape[1]), BF16)
    for h in range(MOBA_HEADS):
        vbt_ref[0, VT_HEAD * h:VT_HEAD * h + HEAD_DIM] = vt[HEAD_DIM * h:HEAD_DIM * (h + 1)].astype(BF16)
        vbt_ref[0, VT_HEAD * h + HEAD_DIM:VT_HEAD * (h + 1)] = ones
    for ref, r0 in ((vst_ref, 512), (vwt_ref, 576)):
        ref[0, 0:HEAD_DIM] = vt[r0:r0 + HEAD_DIM].astype(BF16)
        ref[0, HEAD_DIM:VT_HEAD] = ones


def _rope_tables(S):
    inv = ROPE_THETA ** (-jnp.arange(0, HEAD_DIM, 2, dtype=F32) / HEAD_DIM)
    ang = jnp.arange(S, dtype=F32)[:, None] * inv[None, :]
    cos, sin = jnp.cos(ang), jnp.sin(ang)
    c_head = jnp.concatenate([cos, cos], axis=-1)
    s_head = jnp.concatenate([-sin, sin], axis=-1)
    cq = jnp.tile(c_head, (1, 8))
    sq = jnp.tile(s_head, (1, 8))
    one, zero = jnp.ones_like(c_head), jnp.zeros_like(s_head)
    ckv = jnp.concatenate([c_head, one, c_head, one, c_head, one], axis=-1)
    skv = jnp.concatenate([s_head, zero, s_head, zero, s_head, zero], axis=-1)
    return cq, sq, ckv, skv


def _in_proj(x, mod3, g_attn, w_in):
    B, S, D = x.shape
    tm = SEQ_TILE
    nb = S // tm
    offs = np.cumsum((512, 64, 64, 64, 64, 64, 64, 24, 512, 512))
    cols = jnp.split(w_in, offs.tolist(), axis=1)
    g_w = jnp.pad(cols[7], ((0, 0), (0, LANES - cols[7].shape[1])))
    w = jnp.concatenate([cols[0]] + cols[1:7] + [g_w, cols[8], cols[9]], axis=1).astype(BF16)
    wvt = jnp.concatenate([cols[10], cols[4], cols[6]], axis=1).T.astype(BF16)
    cq, sq, ckv, skv = _rope_tables(S)
    tok = lambda w_: pl.BlockSpec((1, tm, w_), lambda b, i: (b, i, 0))
    tokt = lambda r_: pl.BlockSpec((1, r_, tm), lambda b, i: (b, 0, i))
    tab = lambda w_: pl.BlockSpec((tm, w_), lambda b, i: (i, 0))
    sds = lambda w_, dt: jax.ShapeDtypeStruct((B, S, w_), dt)
    sdst = lambda r_: jax.ShapeDtypeStruct((B, r_, S), BF16)
    return pl.pallas_call(
        _in_proj_kernel,
        grid=(B, nb),
        in_specs=[tok(D),
                  pl.BlockSpec((1, 6, D), lambda b, i: (b, 0, 0)),
                  pl.BlockSpec((1, D), lambda b, i: (0, 0)),
                  pl.BlockSpec((D, _W_COLS), lambda b, i: (0, 0)),
                  pl.BlockSpec((_VT_ROWS, D), lambda b, i: (0, 0)),
                  tab(512), tab(512), tab(384), tab(384)],
        out_specs=[tok(512)] + [tok(HEAD_DIM)] * 4 + [tok(LANES), tok(512), tok(512),
                   pl.BlockSpec((1, 1, 1, 512), lambda b, i: (b, i, 0, 0)),
                   tokt(MOBA_HEADS * VT_HEAD), tokt(VT_HEAD), tokt(VT_HEAD)],
        out_shape=[sds(512, BF16)] + [sds(HEAD_DIM, BF16)] * 4 + [sds(LANES, F32), sds(512, BF16), sds(512, BF16),
                   jax.ShapeDtypeStruct((B, nb, 1, 512), F32), sdst(MOBA_HEADS * VT_HEAD), sdst(VT_HEAD),
                   sdst(VT_HEAD)],
        compiler_params=_cparams(("parallel", "parallel")),
        name="in_proj",
    )(x, mod3, g_attn.reshape(1, D), w, wvt, cq, sq, ckv, skv)


def _compress_kernel(kr_ref, vr_ref, pk_ref, w1k_ref, w2k_ref, pv_ref, w1v_ref, w2vt_ref, ko_ref, vo_ref):
    def hidden(r_ref, p_ref, w1_ref):
        r = r_ref[0]
        half = r.shape[1]
        w1 = w1_ref[...]
        top = _dot(r, w1[:half])
        bot = _dot(r, w1[half:])
        bias = _dot_f32(jnp.broadcast_to(p_ref[...], (SUBLANES, 2 * half)), w1)[0:1]
        n = top.shape[0]
        return _silu(top + pltpu.roll(bot, n - 1, 0) + bias)

    ko_ref[0] = _dot(hidden(kr_ref, pk_ref, w1k_ref), w2k_ref[...]).astype(BF16)
    vo_ref[0] = _dot_nt(w2vt_ref[...], hidden(vr_ref, pv_ref, w1v_ref)).astype(BF16)


def _compress(kc, vc, pos_k, w1_k, w2_k, pos_v, w1_v, w2_v):
    B, S, dh = kc.shape
    ng = S // NSA_CMP_STRIDE
    width = NSA_CMP_STRIDE * dh
    kr = kc.reshape(B, ng, width)
    vr = vc.reshape(B, ng, width)
    full = lambda a: pl.BlockSpec(a.shape, lambda b: (0,) * a.ndim)
    pk = pos_k.reshape(1, NSA_CMP_LEN * dh)
    pv = pos_v.reshape(1, NSA_CMP_LEN * dh)
    args = (kr, vr, pk, w1_k, w2_k, pv, w1_v, w2_v.T)
    return pl.pallas_call(
        _compress_kernel,
        grid=(B,),
        in_specs=[pl.BlockSpec((1, ng, width), lambda b: (b, 0, 0))] * 2 + [full(a) for a in args[2:]],
        out_specs=[pl.BlockSpec((1, ng, dh), lambda b: (b, 0, 0)), pl.BlockSpec((1, dh, ng), lambda b: (b, 0, 0))],
        out_shape=[jax.ShapeDtypeStruct((B, ng, dh), BF16), jax.ShapeDtypeStruct((B, dh, ng), BF16)],
        compiler_params=_cparams(("parallel",)),
        name="nsa_compress",
    )(*args)


def _nsa_cmp_kernel(q_ref, kc_ref, vct_ref, o_ref, selt_ref, *, tq, n_sel):
    qi = pl.program_id(1)
    kc = kc_ref[0]
    vct = vct_ref[0]
    ng = kc.shape[0]
    nidx = lax.broadcasted_iota(I32, (ng, tq), 0)
    t = qi * tq + lax.broadcasted_iota(I32, (ng, tq), 1)
    cmask = (nidx * NSA_CMP_STRIDE + NSA_CMP_LEN - 1 <= t) & (nidx < ng - 1)
    psum = jnp.zeros((ng, tq), F32)
    outs = []
    scores = [_dot_nt(kc, q_ref[0, :, HEAD_DIM * h:HEAD_DIM * (h + 1)]) for h in range(NSA_HEADS)]
    for h in range(NSA_HEADS):
        s = jnp.where(cmask, scores[h], NEG)
        m = jnp.max(s, axis=0, keepdims=True)
        e = jnp.where(cmask, jnp.exp2(s - m), 0.0)
        d = jnp.sum(e, axis=0, keepdims=True)
        p = e / jnp.where(d > 0.0, d, 1.0)
        outs.append(_dot(vct, p).T)
        psum = psum + p
    o_ref[0] = jnp.concatenate(outs, axis=1)
    ss = lax.broadcasted_iota(I32, (n_sel, ng), 0) * NSA_SEL_LEN
    cs = lax.broadcasted_iota(I32, (n_sel, ng), 1) * NSA_CMP_STRIDE
    overlap_t = ((cs < ss + NSA_SEL_LEN) & (cs + NSA_CMP_LEN > ss)).astype(F32)
    imp = _dot_f32(overlap_t, psum)
    j = lax.broadcasted_iota(I32, (n_sel, tq), 0)
    cur = (qi * tq + lax.broadcasted_iota(I32, (n_sel, tq), 1)) // NSA_SEL_LEN
    valid = j <= cur
    forced = valid & ((j == 0) | (j == cur) | (j == cur - 1))
    score = jnp.where(forced, jnp.inf, jnp.where(valid, imp, -jnp.inf))
    sel = _topk_rows(score, min(NSA_SEL_TOPK, n_sel)) & valid
    selt_ref[0] = sel.astype(F32)


def _nsa_cmp(qa, kcmp, vcmp_t):
    B, S, W = qa.shape
    tq = SEQ_TILE
    ng = kcmp.shape[1]
    n_sel = S // NSA_SEL_LEN
    return pl.pallas_call(
        functools.partial(_nsa_cmp_kernel, tq=tq, n_sel=n_sel),
        grid=(B, S // tq),
        in_specs=[pl.BlockSpec((1, tq, W), lambda b, i: (b, i, 0)),
                  pl.BlockSpec((1, ng, HEAD_DIM), lambda b, i: (b, 0, 0)),
                  pl.BlockSpec((1, HEAD_DIM, ng), lambda b, i: (b, 0, 0))],
        out_specs=[pl.BlockSpec((1, tq, W), lambda b, i: (b, i, 0)),
                   pl.BlockSpec((1, n_sel, tq), lambda b, i: (b, 0, i))],
        out_shape=[jax.ShapeDtypeStruct((B, S, W), F32), jax.ShapeDtypeStruct((B, n_sel, S), F32)],
        compiler_params=_cparams(("parallel", "parallel")),
        name="nsa_cmp_select",
    )(qa, kcmp, vcmp_t)


def _nsa_flash_kernel(*refs, mode, tq):
    if mode == "slc":
        q_ref, k_ref, vt_ref, selt_ref, o_ref, qs_ref, m_ref, acc_ref = refs
    else:
        q_ref, k_ref, vt_ref, o_ref, qs_ref, m_ref, acc_ref = refs
    H = NSA_HEADS
    tk = tq
    qi = pl.program_id(1)
    for h in range(H):
        qs_ref[h * tq:(h + 1) * tq, :] = q_ref[0, :, HEAD_DIM * h:HEAD_DIM * (h + 1)]
    m_ref[...] = jnp.full(m_ref.shape, NEG, F32)
    acc_ref[...] = jnp.zeros(acc_ref.shape, F32)
    krow = lax.broadcasted_iota(I32, (tk, tq), 0)
    t_pos = qi * tq + lax.broadcasted_iota(I32, (tk, tq), 1)

    def step(kt):
        k0 = pl.multiple_of(kt * tk, tk)
        k = k_ref[0, pl.ds(k0, tk), :]
        vt = vt_ref[0, :, pl.ds(k0, tk)]
        kpos = k0 + krow
        if mode == "slc":
            n_sel = selt_ref.shape[1]
            kblk = (k0 + lax.broadcasted_iota(I32, (tk, n_sel), 0)) // NSA_SEL_LEN
            expand = (kblk == lax.broadcasted_iota(I32, (tk, n_sel), 1)).astype(BF16)
            mask = (_dot(expand, selt_ref[0]) > 0.5) & (kpos <= t_pos)
        else:
            mask = (kpos <= t_pos) & (kpos > t_pos - NSA_WINDOW)
        group = H // 4
        s_grp = [_dot_nt(k, qs_ref[g * group * tq:(g + 1) * group * tq, :]) for g in range(H // group)]
        for h in range(H):
            hs = slice(h * tq, (h + 1) * tq)
            gs = slice((h % group) * tq, (h % group + 1) * tq)
            s = jnp.where(mask, s_grp[h // group][:, gs], NEG)
            m_prev = m_ref[:, hs]
            m_new = jnp.maximum(m_prev, jnp.max(s, axis=0, keepdims=True))
            alpha = jnp.exp2(m_prev - m_new)
            p = jnp.exp2(s - m_new)
            acc_ref[:, hs] = alpha * acc_ref[:, hs] + _dot(vt, p)
            m_ref[:, hs] = m_new

    step(qi)
    lo = 0 if mode == "slc" else jnp.maximum(qi - NSA_WINDOW // tk, 0)

    def body(kt, carry):
        step(kt)
        return carry

    lax.fori_loop(lo, qi, body, 0)
    o_t = acc_ref[0:HEAD_DIM, :] / acc_ref[HEAD_DIM:HEAD_DIM + 1, :]
    o_ref[0] = jnp.concatenate([o_t[:, h * tq:(h + 1) * tq].T for h in range(H)], axis=1)


def _nsa_flash(mode, qa, k, v_t, sel_t=None):
    B, S, W = qa.shape
    tq = SEQ_TILE
    H = NSA_HEADS
    in_specs = [pl.BlockSpec((1, tq, W), lambda b, i: (b, i, 0)),
                pl.BlockSpec((1, S, HEAD_DIM), lambda b, i: (b, 0, 0)),
                pl.BlockSpec((1, VT_HEAD, S), lambda b, i: (b, 0, 0))]
    args = [qa, k, v_t]
    if mode == "slc":
        in_specs.append(pl.BlockSpec((1, sel_t.shape[1], tq), lambda b, i: (b, 0, i)))
        args.append(sel_t)
    return pl.pallas_call(
        functools.partial(_nsa_flash_kernel, mode=mode, tq=tq),
        grid=(B, S // tq),
        in_specs=in_specs,
        out_specs=pl.BlockSpec((1, tq, W), lambda b, i: (b, i, 0)),
        out_shape=jax.ShapeDtypeStruct((B, S, W), F32),
        scratch_shapes=[pltpu.VMEM((H * tq, HEAD_DIM), BF16),
                        pltpu.VMEM((1, H * tq), F32),
                        pltpu.VMEM((VT_HEAD, H * tq), F32)],
        compiler_params=_cparams(("parallel", "parallel")),
        name="nsa_flash_" + mode,
    )(*args)


def _moba_kernel(q_ref, k_ref, vt_ref, km_ref, o_ref, qs_ref, sel_ref, m_ref, acc_ref, *, nb):
    tq = MOBA_BLOCK
    H = MOBA_HEADS
    own = pl.program_id(1)
    own0 = pl.multiple_of(own * tq, tq)
    krow = lax.broadcasted_iota(I32, (tq, tq), 0)
    qcol = lax.broadcasted_iota(I32, (tq, tq), 1)
    blk = lax.broadcasted_iota(I32, (nb, tq), 0)
    heads = [slice(HEAD_DIM * h, HEAD_DIM * (h + 1)) for h in range(H)]
    vheads = [slice(VT_HEAD * h, VT_HEAD * (h + 1)) for h in range(H)]
    for h, sl in enumerate(heads):
        qs_ref[h] = q_ref[0, :, sl]
    gates = [_dot_nt_f32(km_ref[0, :, sl], qs_ref[h]) for h, sl in enumerate(heads)]
    scores = [_dot_nt(k_ref[0, pl.ds(own0, tq), sl], qs_ref[h]) for h, sl in enumerate(heads)]
    for h in range(H):
        gate = jnp.where(blk < own, gates[h], -jnp.inf)
        sel_ref[h] = (_topk_rows(gate, min(MOBA_TOPK, nb)) & (blk < own)).astype(F32)
        s = jnp.where(krow <= qcol, scores[h], NEG)
        m = jnp.max(s, axis=0, keepdims=True)
        m_ref[h] = m
        acc_ref[h] = _dot(vt_ref[0, vheads[h], pl.ds(own0, tq)], jnp.exp2(s - m))

    def body(n, carry):
        n0 = pl.multiple_of(n * tq, tq)
        scores = [_dot_nt(k_ref[0, pl.ds(n0, tq), sl], qs_ref[h]) for h, sl in enumerate(heads)]
        for h in range(H):
            chosen = sel_ref[h, pl.ds(n, 1), :] > 0.5
            s = scores[h]
            m_prev = m_ref[h]
            m_new = jnp.where(chosen, jnp.maximum(m_prev, jnp.max(s, axis=0, keepdims=True)), m_prev)
            alpha = jnp.exp2(m_prev - m_new)
            p = jnp.exp2(s - m_new)
            pv = _dot(vt_ref[0, vheads[h], pl.ds(n0, tq)], p)
            acc_ref[h] = alpha * acc_ref[h] + jnp.where(chosen, pv, 0.0)
            m_ref[h] = m_new
        return carry

    lax.fori_loop(0, own, body, 0)
    o_ref[0] = jnp.concatenate([(acc_ref[h, 0:HEAD_DIM] / acc_ref[h, HEAD_DIM:HEAD_DIM + 1]).T for h in range(H)],
                               axis=1)


def _moba(qb, kb, vb_t, kmean):
    B, S, W = qb.shape
    tq = MOBA_BLOCK
    nb = S // tq
    H = MOBA_HEADS
    return pl.pallas_call(
        functools.partial(_moba_kernel, nb=nb),
        grid=(B, nb),
        in_specs=[pl.BlockSpec((1, tq, W), lambda b, i: (b, i, 0)),
                  pl.BlockSpec((1, S, W), lambda b, i: (b, 0, 0)),
                  pl.BlockSpec((1, H * VT_HEAD, S), lambda b, i: (b, 0, 0)),
                  pl.BlockSpec((1, nb, W), lambda b, i: (b, 0, 0))],
        out_specs=pl.BlockSpec((1, tq, W), lambda b, i: (b, i, 0)),
        out_shape=jax.ShapeDtypeStruct((B, S, W), F32),
        scratch_shapes=[pltpu.VMEM((H, tq, HEAD_DIM), BF16),
                        pltpu.VMEM((H, nb, tq), F32),
                        pltpu.VMEM((H, 1, tq), F32),
                        pltpu.VMEM((H, VT_HEAD, tq), F32)],
        compiler_params=_cparams(("parallel", "parallel")),
        name="moba",
    )(qb, kb, vb_t, kmean)


def _post_attn_kernel(x_ref, mod_ref, oc_ref, os_ref, ow_ref, gt_ref, ob_ref, wo_ref, gf_ref, wrt_ref, rb_ref,
                      wsg_ref, wsu_ref, wsd_ref, h2_ref, pre_ref, eidx_ref, wts_ref, hot_ref, cnt_ref):
    first = (pl.program_id(0) == 0) & (pl.program_id(1) == 0)
    x = x_ref[0]
    tm = x.shape[0]
    gt1 = mod_ref[0, 2:3, :]
    sh2 = mod_ref[0, 3:4, :]
    sc2 = mod_ref[0, 4:5, :]
    gt2 = mod_ref[0, 5:6, :]
    gates = gt_ref[0]
    gi = lax.broadcasted_iota(I32, (LANES, 512), 0)
    hi = lax.broadcasted_iota(I32, (LANES, 512), 1) // HEAD_DIM
    o_a = jnp.zeros((tm, 512), F32)
    for jbr, ref in enumerate((oc_ref, os_ref, ow_ref)):
        expand = (gi == 3 * hi + jbr).astype(F32)
        o_a = o_a + _dot_f32(gates, expand) * ref[0]
    wo = wo_ref[...]
    attn = _dot(o_a, wo[:512]) + _dot(ob_ref[0], wo[512:])
    h1 = x + gt1 * attn
    h2 = (_rms(h1, gf_ref[...]) * (1.0 + sc2) + sh2).astype(BF16)
    half = h2.shape[1] // 2
    lo = lax.bitcast_convert_type(h2[:, :half].astype(F32), U32) >> 16
    hi = lax.bitcast_convert_type(h2[:, half:].astype(F32), U32) & jnp.uint32(0xFFFF0000)
    words = lo | hi
    for s in range(PACKED_SUBLANES):
        h2_ref[pl.ds(s, tm, stride=PACKED_SUBLANES), :] = words[:, LANES * s:LANES * (s + 1)]
    shared = _dot(_silu(_dot(h2, wsg_ref[...])) * _dot(h2, wsu_ref[...]), wsd_ref[...])
    pre = h1 + gt2 * shared
    for s in range(SUBLANES):
        pre_ref[pl.ds(s, tm, stride=SUBLANES), :] = pre[:, LANES * s:LANES * (s + 1)]
    scores = jax.nn.sigmoid(_dot_nt(wrt_ref[...], h2))
    biased = scores + rb_ref[...]
    E = N_EXPERTS
    per = E // N_GROUPS
    rowp = lax.broadcasted_iota(I32, (per, tm), 0)
    gs, vals = [], []
    for g in range(N_GROUPS):
        v = biased[per * g:per * (g + 1), :]
        m1 = jnp.max(v, axis=0, keepdims=True)
        i1 = jnp.min(jnp.where(v == m1, rowp, per), axis=0, keepdims=True)
        m2 = jnp.max(jnp.where(rowp == i1, -jnp.inf, v), axis=0, keepdims=True)
        gs.append(m1 + m2)
        vals.append(v)
    cands = []
    for g in range(N_GROUPS):
        rank = jnp.zeros((1, tm), I32)
        for i in range(N_GROUPS):
            beats = (gs[i] > gs[g]) | ((gs[i] == gs[g]) & (i < g))
            rank = rank + beats.astype(I32)
        cands.append(jnp.where(rank < TOPK_GROUPS, vals[g], -jnp.inf))
    cand = jnp.concatenate(cands, axis=0)
    row = lax.broadcasted_iota(I32, (E, tm), 0)
    krow = lax.broadcasted_iota(I32, (TOP_K, tm), 0)
    eidx = jnp.zeros((TOP_K, tm), I32)
    wsel = jnp.zeros((TOP_K, tm), F32)
    hot = jnp.zeros((E, tm), F32)
    wsum = jnp.zeros((1, tm), F32)
    for kk in range(TOP_K):
        m = jnp.max(cand, axis=0, keepdims=True)
        idx = jnp.min(jnp.where(cand == m, row, E), axis=0, keepdims=True)
        pick = row == idx
        w = jnp.sum(jnp.where(pick, scores, 0.0), axis=0, keepdims=True)
        cand = jnp.where(pick, -jnp.inf, cand)
        hot = hot + pick.astype(F32)
        eidx = jnp.where(krow == kk, idx, eidx)
        wsel = jnp.where(krow == kk, w, wsel)
        wsum = wsum + w
    eidx_ref[...] = eidx
    wts_ref[...] = wsel / wsum * ROUTED_SCALE
    hot_ref[...] = hot.astype(BF16)

    @pl.when(first)
    def _():
        cnt_ref[...] = jnp.zeros(cnt_ref.shape, F32)

    cnt_ref[...] += jnp.sum(hot, axis=1, keepdims=True)


def _post_attn(x, mod3, o_cmp, o_slc, o_win, gates, o_b, w_out, g_ffn, w_router, router_bias, w_sg, w_su, w_sd):
    B, S, D = x.shape
    tm = SEQ_TILE
    nt = S // tm
    T = B * S
    E = N_EXPERTS
    tok = lambda w_: pl.BlockSpec((1, tm, w_), lambda b, i: (b, i, 0))
    full = lambda a: pl.BlockSpec(a.shape, lambda b, i: (0,) * a.ndim)
    rows = lambda r_: pl.BlockSpec((tm * r_, LANES), lambda b, i: (b * nt + i, 0))
    cols = lambda r_: pl.BlockSpec((r_, tm), lambda b, i: (0, b * nt + i))
    consts = (w_out.astype(BF16), g_ffn.reshape(1, D), w_router.T.astype(BF16), router_bias.reshape(E, 1),
              w_sg.astype(BF16), w_su.astype(BF16), w_sd.astype(BF16))
    return pl.pallas_call(
        _post_attn_kernel,
        grid=(B, nt),
        in_specs=[tok(D), pl.BlockSpec((1, 6, D), lambda b, i: (b, 0, 0)),
                  tok(512), tok(512), tok(512), tok(LANES), tok(512)] + [full(a) for a in consts],
        out_specs=[rows(PACKED_SUBLANES), rows(SUBLANES), cols(TOP_K), cols(TOP_K), cols(E),
                   pl.BlockSpec((E, 1), lambda b, i: (0, 0))],
        out_shape=[jax.ShapeDtypeStruct((T * PACKED_SUBLANES, LANES), U32),
                   jax.ShapeDtypeStruct((T * SUBLANES, LANES), F32),
                   jax.ShapeDtypeStruct((TOP_K, T), I32), jax.ShapeDtypeStruct((TOP_K, T), F32),
                   jax.ShapeDtypeStruct((E, T), BF16), jax.ShapeDtypeStruct((E, 1), F32)],
        compiler_params=_cparams(("arbitrary", "arbitrary")),
        name="post_attn_router",
    )(x, mod3, o_cmp, o_slc, o_win, gates, o_b, *consts)


def _moe_pos_kernel(hot_ref, eidx_ref, start_ref, dest_ref, carry_ref):
    @pl.when(pl.program_id(0) == 0)
    def _():
        carry_ref[...] = start_ref[...]

    hot = hot_ref[...]
    E, tp = hot.shape
    r = lax.broadcasted_iota(I32, (tp, tp), 0)
    c = lax.broadcasted_iota(I32, (tp, tp), 1)
    earlier = (r < c).astype(BF16)
    pos = carry_ref[...] + _dot(hot, earlier)
    row = lax.broadcasted_iota(I32, (E, tp), 0)
    krow = lax.broadcasted_iota(I32, (TOP_K, tp), 0)
    eidx = eidx_ref[...]
    dest = jnp.zeros((TOP_K, tp), I32)
    for kk in range(TOP_K):
        d = jnp.sum(jnp.where(row == eidx[kk:kk + 1, :], pos, 0.0), axis=0, keepdims=True)
        dest = jnp.where(krow == kk, d.astype(I32), dest)
    dest_ref[...] = dest
    carry_ref[...] += jnp.sum(hot.astype(F32), axis=1, keepdims=True)


def _moe_pos(hot, eidx, starts):
    E, T = hot.shape
    tp = 512 if T % 512 == 0 else SEQ_TILE
    return pl.pallas_call(
        _moe_pos_kernel,
        grid=(T // tp,),
        in_specs=[pl.BlockSpec((E, tp), lambda i: (0, i)),
                  pl.BlockSpec((TOP_K, tp), lambda i: (0, i)),
                  pl.BlockSpec((E, 1), lambda i: (0, 0))],
        out_specs=pl.BlockSpec((TOP_K, tp), lambda i: (0, i)),
        out_shape=jax.ShapeDtypeStruct((TOP_K, T), I32),
        scratch_shapes=[pltpu.VMEM((E, 1), F32)],
        compiler_params=_cparams(("arbitrary",)),
        name="moe_pos",
    )(hot, eidx, starts)


def _row(ref, start, n=SUBLANES):
    return ref.at[pl.ds(pl.multiple_of(start, n), n)]


def _moe_scatter_kernel(dest_ref, h_ref, zero_ref, xs_ref, sem):
    del zero_ref
    n = PACKED_SUBLANES
    ts = h_ref.shape[0] // n

    def copy(r, kk):
        return pltpu.make_async_copy(_row(h_ref, r * n, n), xs_ref.at[dest_ref[r * TOP_K + kk]], sem)

    def issue(r, carry):
        for kk in range(TOP_K):
            copy(r, kk).start(priority=kk % 2)
        return carry

    def drain(r, carry):
        for kk in range(TOP_K):
            copy(r, kk).wait()
        return carry

    lax.fori_loop(0, ts, issue, 0)
    lax.fori_loop(0, ts, drain, 0)


def _moe_scatter(h2p, dest4, n_rows):
    n = PACKED_SUBLANES
    T = h2p.shape[0] // n
    ts = MOE_TILE
    zeros = jnp.zeros((n_rows, n, LANES), U32)
    return pl.pallas_call(
        _moe_scatter_kernel,
        grid=(T // ts,),
        in_specs=[pl.BlockSpec((ts * TOP_K,), lambda i: (i,), memory_space=pltpu.SMEM),
                  pl.BlockSpec((ts * n, LANES), lambda i: (i, 0)),
                  pl.BlockSpec(memory_space=pl.ANY)],
        out_specs=pl.BlockSpec(memory_space=pl.ANY),
        out_shape=jax.ShapeDtypeStruct((n_rows, n, LANES), U32),
        scratch_shapes=[pltpu.SemaphoreType.DMA(())],
        input_output_aliases={2: 0},
        compiler_params=_cparams(("arbitrary",)),
        name="moe_scatter",
    )(dest4, h2p, zeros)


def _moe_experts_kernel(be_ref, bf_ref, nx_ref, wp_ref, nu_ref, xs_ref, wg_ref, wu_ref, wd_ref, ys_ref,
                        wg_s, wu_s, wd_s, wg_f, wu_f, wd_f, xbuf, ybuf, sem_in, sem_out, sem_w):
    i = pl.program_id(0)
    n_used = nu_ref[0]
    rows = EXPERT_ROWS
    slot = i % 2
    islot = i % IN_SLOTS

    def w_copies(e, sl):
        return [pltpu.make_async_copy(src.at[e], dst.at[sl], sem_w.at[sl])
                for src, dst in ((wg_ref, wg_f), (wu_ref, wu_f), (wd_ref, wd_f))]

    def in_copies(blk):
        r0 = pl.multiple_of(blk * rows, rows)
        sl = blk % IN_SLOTS
        return [pltpu.make_async_copy(xs_ref.at[pl.ds(r0, rows), s], xbuf.at[sl, s], sem_in.at[sl])
                for s in range(PACKED_SUBLANES)]

    def out_copies(blk, sl):
        r0 = pl.multiple_of(blk * rows, rows)
        return [pltpu.make_async_copy(ybuf.at[sl, s], ys_ref.at[pl.ds(r0, rows), s], sem_out.at[sl])
                for s in range(SUBLANES)]

    @pl.when(i == 0)
    def _():
        for cp in w_copies(be_ref[0], 0):
            cp.start()
        for ahead in range(IN_SLOTS - 1):
            @pl.when(ahead < n_used)
            def _():
                for cp in in_copies(ahead):
                    cp.start()

    @pl.when(i + IN_SLOTS - 1 < n_used)
    def _():
        for cp in in_copies(i + IN_SLOTS - 1):
            cp.start()

    @pl.when((bf_ref[i] == 1) & (i < n_used))
    def _():
        wslot = wp_ref[i]
        for cp in w_copies(be_ref[i], wslot):
            cp.wait()
        wg_s[...] = wg_f[wslot].astype(BF16)
        wu_s[...] = wu_f[wslot].astype(BF16)
        wd_s[...] = wd_f[wslot].astype(BF16)

        @pl.when(nx_ref[i] >= 0)
        def _():
            for cp in w_copies(nx_ref[i], 1 - wslot):
                cp.start()

    @pl.when(i < n_used)
    def _():
        for cp in in_copies(i):
            cp.wait()

        @pl.when(i >= 2)
        def _():
            for cp in out_copies(i - 2, slot):
                cp.wait()

        words = [xbuf[islot, s] for s in range(PACKED_SUBLANES)]
        lo = [lax.bitcast_convert_type(w << 16, F32).astype(BF16) for w in words]
        hi = [lax.bitcast_convert_type(w & jnp.uint32(0xFFFF0000), F32).astype(BF16) for w in words]
        x = jnp.concatenate(lo + hi, axis=1)
        a = _silu(_dot(x, wg_s[...])) * _dot(x, wu_s[...])
        y = _dot(a, wd_s[...])
        for s in range(SUBLANES):
            ybuf[slot, s] = y[:, LANES * s:LANES * (s + 1)]
        for cp in out_copies(i, slot):
            cp.start()

        @pl.when(i == n_used - 1)
        def _():
            for cp in out_copies(i, slot):
                cp.wait()

            @pl.when(i >= 1)
            def _():
                for cp in out_copies(i - 1, 1 - slot):
                    cp.wait()


def _moe_experts(xs, blk_exp, blk_first, blk_next, blk_wslot, n_used, w_eg, w_eu, w_ed):
    n_rows = xs.shape[0]
    E, D, De = w_eg.shape
    n_blk = n_rows // EXPERT_ROWS
    grid_spec = pltpu.PrefetchScalarGridSpec(
        num_scalar_prefetch=5,
        grid=(n_blk,),
        in_specs=[pl.BlockSpec(memory_space=pl.ANY)] * 4,
        out_specs=pl.BlockSpec(memory_space=pl.ANY),
        scratch_shapes=[pltpu.VMEM((D, De), BF16), pltpu.VMEM((D, De), BF16), pltpu.VMEM((De, D), BF16),
                        pltpu.VMEM((2, D, De), F32), pltpu.VMEM((2, D, De), F32), pltpu.VMEM((2, De, D), F32),
                        pltpu.VMEM((IN_SLOTS, PACKED_SUBLANES, EXPERT_ROWS, LANES), U32),
                        pltpu.VMEM((2, SUBLANES, EXPERT_ROWS, LANES), F32),
                        pltpu.SemaphoreType.DMA((IN_SLOTS,)), pltpu.SemaphoreType.DMA((2,)),
                        pltpu.SemaphoreType.DMA((2,))],
    )
    return pl.pallas_call(
        _moe_experts_kernel,
        grid_spec=grid_spec,
        out_shape=jax.ShapeDtypeStruct((n_rows, SUBLANES, LANES), F32),
        compiler_params=_cparams(("arbitrary",)),
        name="moe_experts",
    )(blk_exp, blk_first, blk_next, blk_wslot, n_used, xs, w_eg, w_eu, w_ed)


def _moe_combine_kernel(dcur_ref, dnxt_ref, w_ref, pre_ref, gt2_ref, gfin_ref, ys_ref, o_ref, buf, hbuf, sem):
    i = pl.program_id(0)
    n = pl.num_programs(0)
    tc = pre_ref.shape[0] // SUBLANES
    slot = i % 2

    def copy(d_ref, sl, j):
        return pltpu.make_async_copy(ys_ref.at[d_ref[j]], _row(buf.at[sl], j * SUBLANES), sem.at[sl])

    def issue(d_ref, sl):
        def body(r, carry):
            for kk in range(TOP_K):
                copy(d_ref, sl, r * TOP_K + kk).start(priority=kk % 2)
            return carry
        lax.fori_loop(0, tc, body, 0)

    @pl.when(i == 0)
    def _():
        issue(dcur_ref, 0)

    @pl.when(i + 1 < n)
    def _():
        issue(dnxt_ref, 1 - slot)

    def drain(r, carry):
        for kk in range(TOP_K):
            copy(dcur_ref, slot, r * TOP_K + kk).wait()
        return carry

    lax.fori_loop(0, tc, drain, 0)
    cur = buf.at[slot]
    gt2 = gt2_ref[...]
    unroll = 4

    def tokens(g, carry):
        for u in range(unroll):
            r = g * unroll + u
            routed = jnp.zeros((SUBLANES, LANES), F32)
            for kk in range(TOP_K):
                j = r * TOP_K + kk
                routed = routed + w_ref[j] * _row(cur, j * SUBLANES)[...]
            _row(hbuf, r * SUBLANES)[...] = _row(pre_ref, r * SUBLANES)[...] + gt2 * routed
        return carry

    lax.fori_loop(0, tc // unroll, tokens, 0)
    h = jnp.concatenate([hbuf[pl.ds(s, tc, stride=SUBLANES), :] for s in range(SUBLANES)], axis=1)
    o_ref[...] = _rms(h, gfin_ref[...])


def _moe_combine(ys, dest8, w_flat, pre2d, gt2_2d, gfin, S):
    T = pre2d.shape[0] // SUBLANES
    tc = MOE_TILE
    per_b = S // tc
    n = T // tc
    return pl.pallas_call(
        _moe_combine_kernel,
        grid=(n,),
        in_specs=[pl.BlockSpec((tc * TOP_K,), lambda i: (i,), memory_space=pltpu.SMEM),
                  pl.BlockSpec((tc * TOP_K,), lambda i: (jnp.minimum(i + 1, n - 1),), memory_space=pltpu.SMEM),
                  pl.BlockSpec((tc * TOP_K,), lambda i: (i,), memory_space=pltpu.SMEM),
                  pl.BlockSpec((tc * SUBLANES, LANES), lambda i: (i, 0)),
                  pl.BlockSpec((SUBLANES, LANES), lambda i: (i // per_b, 0)),
                  pl.BlockSpec((1, SUBLANES * LANES), lambda i: (0, 0)),
                  pl.BlockSpec(memory_space=pl.ANY)],
        out_specs=pl.BlockSpec((tc, SUBLANES * LANES), lambda i: (i, 0)),
        out_shape=jax.ShapeDtypeStruct((T, SUBLANES * LANES), F32),
        scratch_shapes=[pltpu.VMEM((2, tc * TOP_K * SUBLANES, LANES), F32),
                        pltpu.VMEM((tc * SUBLANES, LANES), F32),
                        pltpu.SemaphoreType.DMA((2,))],
        compiler_params=_cparams(("arbitrary",)),
        name="moe_combine",
    )(dest8, dest8, w_flat, pre2d, gt2_2d, gfin, ys)


def _layer(h_res, mod, g_attn, w_in, cmp_pos_k, cmp_w1_k, cmp_w2_k, cmp_pos_v, cmp_w1_v, cmp_w2_v,
           w_out, g_ffn, w_router, router_bias, w_eg, w_eu, w_ed, w_sg, w_su, w_sd):
    B, S, D = h_res.shape
    T = B * S
    mod3 = mod.reshape(B, 6, D)
    (qa, kc, vc, ks, kw, gates, qb, kb, kmean, vb_t, vs_t, vw_t) = _in_proj(h_res, mod3, g_attn, w_in)
    kcmp, vcmp_t = _compress(kc, vc, cmp_pos_k, cmp_w1_k, cmp_w2_k, cmp_pos_v, cmp_w1_v, cmp_w2_v)
    o_cmp, sel_t = _nsa_cmp(qa, kcmp, vcmp_t)
    o_slc = _nsa_flash("slc", qa, ks, vs_t, sel_t)
    o_win = _nsa_flash("win", qa, kw, vw_t)
    o_b = _moba(qb, kb, vb_t, kmean.reshape(B, S // MOBA_BLOCK, 512))
    h2p, pre2d, eidx, wts, hot, counts = _post_attn(h_res, mod3, o_cmp, o_slc, o_win, gates, o_b, w_out, g_ffn,
                                                    w_router, router_bias, w_sg, w_su, w_sd)
    counts = counts.reshape(N_EXPERTS).astype(I32)
    padded = (counts + EXPERT_ROWS - 1) // EXPERT_ROWS * EXPERT_ROWS
    ends = jnp.cumsum(padded)
    starts = ends - padded
    n_blk = -(-(T * TOP_K) // EXPERT_ROWS) + N_EXPERTS
    n_rows = n_blk * EXPERT_ROWS
    blk_row = jnp.arange(n_blk, dtype=I32) * EXPERT_ROWS
    blk_exp = jnp.minimum(jnp.sum((ends[None, :] <= blk_row[:, None]).astype(I32), axis=1), N_EXPERTS - 1)
    blk_first = jnp.concatenate([jnp.ones((1,), I32), (blk_exp[1:] != blk_exp[:-1]).astype(I32)])
    n_used = (ends[-1:] // EXPERT_ROWS).astype(I32)
    blk_idx = jnp.arange(n_blk, dtype=I32)
    first_used = (blk_first == 1) & (blk_idx < n_used[0])
    first_pos = jnp.where(first_used, blk_idx, n_blk)
    next_pos = jnp.concatenate([lax.cummin(first_pos, reverse=True)[1:], jnp.full((1,), n_blk, I32)])
    blk_next = jnp.where(next_pos < n_blk, blk_exp[jnp.minimum(next_pos, n_blk - 1)], -1).astype(I32)
    blk_wslot = ((jnp.cumsum(first_used.astype(I32)) - 1) % 2).astype(I32)
    dest = _moe_pos(hot, eidx, starts.astype(F32).reshape(N_EXPERTS, 1))
    dest_flat = dest.T.reshape(T * TOP_K)
    w_flat = wts.T.reshape(T * TOP_K)
    xs = _moe_scatter(h2p, dest_flat, n_rows)
    ys = _moe_experts(xs, blk_exp, blk_first, blk_next, blk_wslot, n_used, w_eg, w_eu, w_ed)
    gt2_2d = mod3[:, 5, :].reshape(B * SUBLANES, LANES)
    return ys, dest_flat, w_flat, pre2d, gt2_2d


def kernel(x, c, w_ada, b_ada, g_attn, w_in, cmp_pos_k, cmp_w1_k, cmp_w2_k, cmp_pos_v, cmp_w1_v, cmp_w2_v, w_out, g_ffn, w_router, router_bias, w_exp_gate, w_exp_up, w_exp_down, w_sh_gate, w_sh_up, w_sh_down, g_final):
    B, S, D = x.shape
    depth = w_ada.shape[0]
    assert depth == 1, "the final RMSNorm is fused into the single layer's MoE combine"
    assert D == SUBLANES * LANES and S % SEQ_TILE == 0
    l = 0
    mod = _ada_mod(c.astype(F32), w_ada[l], b_ada[l])
    ys, dest8, w_flat, pre2d, gt2_2d = _layer(
        x.astype(F32), mod, g_attn[l], w_in[l], cmp_pos_k[l], cmp_w1_k[l], cmp_w2_k[l], cmp_pos_v[l], cmp_w1_v[l],
        cmp_w2_v[l], w_out[l], g_ffn[l], w_router[l], router_bias[l], w_exp_gate[l], w_exp_up[l], w_exp_down[l],
        w_sh_gate[l], w_sh_up[l], w_sh_down[l])
    out = _moe_combine(ys, dest8, w_flat, pre2d, gt2_2d, g_final.reshape(1, D).astype(F32), S)
    return out.reshape(B, S, D).astype(x.dtype)
```

```python
import functools

import jax
import jax.numpy as jnp
import numpy as np
from jax import lax
from jax.experimental import pallas as pl
from jax.experimental.pallas import tpu as pltpu

F32 = jnp.float32
BF16 = jnp.bfloat16
I32 = jnp.int32
U32 = jnp.uint32

HEAD_DIM = 64
HALF = HEAD_DIM // 2
NSA_HEADS = 8
NSA_CMP_LEN = 32
NSA_CMP_STRIDE = 16
NSA_SEL_LEN = 64
NSA_SEL_TOPK = 16
NSA_WINDOW = 512
MOBA_HEADS = 8
MOBA_BLOCK = 256
MOBA_TOPK = 3
N_EXPERTS = 256
TOP_K = 8
N_GROUPS = 8
TOPK_GROUPS = 4
ROUTED_SCALE = 2.5
ROPE_THETA = 10000.0
EPS = 1e-6
NEG = -1e30
LOG2_E = 1.4426950408889634
ONES_ROWS = 16
VT_HEAD = HEAD_DIM + ONES_ROWS

LANES = 128
SUBLANES = 8
PACKED_SUBLANES = SUBLANES // 2
VMEM_LIMIT = 56 * 1024 * 1024
EXPERT_ROWS = 256
SEQ_TILE = 256
MOE_TILE = 128
IN_SLOTS = 4


def _cparams(sem):
    return pltpu.CompilerParams(dimension_semantics=sem, vmem_limit_bytes=VMEM_LIMIT)


def _dot(a, b):
    return jnp.dot(a.astype(BF16), b.astype(BF16), preferred_element_type=F32)


def _dot_nt(a, b):
    return lax.dot_general(a.astype(BF16), b.astype(BF16), (((1,), (1,)), ((), ())),
                           preferred_element_type=F32)


def _dot_f32(a, b):
    return jnp.dot(a.astype(F32), b.astype(F32), preferred_element_type=F32)


def _dot_nt_f32(a, b):
    return lax.dot_general(a.astype(F32), b.astype(F32), (((1,), (1,)), ((), ())),
                           preferred_element_type=F32)


def _silu(x):
    return x * jax.nn.sigmoid(x)


def _rms(x, g):
    return x * lax.rsqrt(jnp.mean(x * x, axis=-1, keepdims=True) + EPS) * g


def _topk_rows(score, k):
    n = score.shape[0]
    idx = lax.broadcasted_iota(I32, score.shape, 0)
    rank = jnp.zeros(score.shape, I32)
    for i in range(n):
        row = score[i:i + 1, :]
        beats = (row > score) | ((row == score) & (idx > i))
        rank = rank + beats.astype(I32)
    return rank < k


def _ada_kernel(c_ref, w_ref, b_ref, o_ref):
    o_ref[...] = _dot_f32(_silu(c_ref[...]), w_ref[...]) + b_ref[...]


def _ada_mod(c, w, b):
    B, D = c.shape
    N = w.shape[1]
    tn = 1536 if N % 1536 == 0 else N
    return pl.pallas_call(
        _ada_kernel,
        grid=(N // tn,),
        in_specs=[pl.BlockSpec((B, D), lambda j: (0, 0)),
                  pl.BlockSpec((D, tn), lambda j: (0, j)),
                  pl.BlockSpec((1, tn), lambda j: (0, j))],
        out_specs=pl.BlockSpec((B, tn), lambda j: (0, j)),
        out_shape=jax.ShapeDtypeStruct((B, N), F32),
        compiler_params=_cparams(("arbitrary",)),
        name="ada_mod",
    )(c, w, b.reshape(1, N))


_OFF_QA, _OFF_KV, _OFF_G, _OFF_QB, _OFF_KB, _W_COLS = 0, 512, 896, 1024, 1536, 2048
_VT_ROWS = 640


def _swap_halves(t):
    n = t.shape[-1]
    lane = lax.broadcasted_iota(I32, t.shape, t.ndim - 1)
    left = pltpu.roll(t, n - HALF, t.ndim - 1)
    right = pltpu.roll(t, HALF, t.ndim - 1)
    return jnp.where((lane % HEAD_DIM) < HALF, left, right)


def _in_proj_kernel(x_ref, mod_ref, g_ref, w_ref, wvt_ref, cq_ref, sq_ref, ckv_ref, skv_ref,
                    qa_ref, kc_ref, vc_ref, ks_ref, kw_ref, gt_ref, qb_ref, kb_ref, km_ref,
                    vbt_ref, vst_ref, vwt_ref):
    x = x_ref[0]
    sh1 = mod_ref[0, 0:1, :]
    sc1 = mod_ref[0, 1:2, :]
    h = (_rms(x, g_ref[...]) * (1.0 + sc1) + sh1).astype(BF16)
    proj = _dot(h, w_ref[...])
    cq, sq = cq_ref[...], sq_ref[...]
    scale = HEAD_DIM ** -0.5 * LOG2_E

    def rope(t, c, s):
        return t * c + _swap_halves(t) * s

    qa = rope(proj[:, _OFF_QA:_OFF_QA + 512], cq, sq) * scale
    qa_ref[0] = qa.astype(BF16)
    kv = rope(proj[:, _OFF_KV:_OFF_KV + 384], ckv_ref[...], skv_ref[...])
    for j, ref in ((0, kc_ref), (1, vc_ref), (2, ks_ref), (4, kw_ref)):
        ref[0] = kv[:, HEAD_DIM * j:HEAD_DIM * (j + 1)].astype(BF16)
    gt_ref[0] = jax.nn.sigmoid(proj[:, _OFF_G:_OFF_G + LANES])
    qb = rope(proj[:, _OFF_QB:_OFF_QB + 512], cq, sq) * scale
    qb_ref[0] = qb.astype(BF16)
    kb = rope(proj[:, _OFF_KB:_OFF_KB + 512], cq, sq)
    kb_ref[0] = kb.astype(BF16)
    km_ref[0, 0] = jnp.mean(kb, axis=0, keepdims=True)
    vt = _dot_nt(wvt_ref[...], h)
    ones = jnp.ones((ONES_ROWS, vt.shape[1]), BF16)
    for h in range(MOBA_HEADS):
        vbt_ref[0, VT_HEAD * h:VT_HEAD * h + HEAD_DIM] = vt[HEAD_DIM * h:HEAD_DIM * (h + 1)].astype(BF16)
        vbt_ref[0, VT_HEAD * h + HEAD_DIM:VT_HEAD * (h + 1)] = ones
    for ref, r0 in ((vst_ref, 512), (vwt_ref, 576)):
        ref[0, 0:HEAD_DIM] = vt[r0:r0 + HEAD_DIM].astype(BF16)
        ref[0, HEAD_DIM:VT_HEAD] = ones


def _rope_tables(S):
    inv = ROPE_THETA ** (-jnp.arange(0, HEAD_DIM, 2, dtype=F32) / HEAD_DIM)
    ang = jnp.arange(S, dtype=F32)[:, None] * inv[None, :]
    cos, sin = jnp.cos(ang), jnp.sin(ang)
    c_head = jnp.concatenate([cos, cos], axis=-1)
    s_head = jnp.concatenate([-sin, sin], axis=-1)
    cq = jnp.tile(c_head, (1, 8))
    sq = jnp.tile(s_head, (1, 8))
    one, zero = jnp.ones_like(c_head), jnp.zeros_like(s_head)
    ckv = jnp.concatenate([c_head, one, c_head, one, c_head, one], axis=-1)
    skv = jnp.concatenate([s_head, zero, s_head, zero, s_head, zero], axis=-1)
    return cq, sq, ckv, skv


def _in_proj(x, mod3, g_attn, w_in):
    B, S, D = x.shape
    tm = SEQ_TILE
    nb = S // tm
    offs = np.cumsum((512, 64, 64, 64, 64, 64, 64, 24, 512, 512))
    cols = jnp.split(w_in, offs.tolist(), axis=1)
    g_w = jnp.pad(cols[7], ((0, 0), (0, LANES - cols[7].shape[1])))
    w = jnp.concatenate([cols[0]] + cols[1:7] + [g_w, cols[8], cols[9]], axis=1).astype(BF16)
    wvt = jnp.concatenate([cols[10], cols[4], cols[6]], axis=1).T.astype(BF16)
    cq, sq, ckv, skv = _rope_tables(S)
    tok = lambda w_: pl.BlockSpec((1, tm, w_), lambda b, i: (b, i, 0))
    tokt = lambda r_: pl.BlockSpec((1, r_, tm), lambda b, i: (b, 0, i))
    tab = lambda w_: pl.BlockSpec((tm, w_), lambda b, i: (i, 0))
    sds = lambda w_, dt: jax.ShapeDtypeStruct((B, S, w_), dt)
    sdst = lambda r_: jax.ShapeDtypeStruct((B, r_, S), BF16)
    return pl.pallas_call(
        _in_proj_kernel,
        grid=(B, nb),
        in_specs=[tok(D),
                  pl.BlockSpec((1, 6, D), lambda b, i: (b, 0, 0)),
                  pl.BlockSpec((1, D), lambda b, i: (0, 0)),
                  pl.BlockSpec((D, _W_COLS), lambda b, i: (0, 0)),
                  pl.BlockSpec((_VT_ROWS, D), lambda b, i: (0, 0)),
                  tab(512), tab(512), tab(384), tab(384)],
        out_specs=[tok(512)] + [tok(HEAD_DIM)] * 4 + [tok(LANES), tok(512), tok(512),
                   pl.BlockSpec((1, 1, 1, 512), lambda b, i: (b, i, 0, 0)),
                   tokt(MOBA_HEADS * VT_HEAD), tokt(VT_HEAD), tokt(VT_HEAD)],
        out_shape=[sds(512, BF16)] + [sds(HEAD_DIM, BF16)] * 4 + [sds(LANES, F32), sds(512, BF16), sds(512, BF16),
                   jax.ShapeDtypeStruct((B, nb, 1, 512), F32), sdst(MOBA_HEADS * VT_HEAD), sdst(VT_HEAD),
                   sdst(VT_HEAD)],
        compiler_params=_cparams(("parallel", "parallel")),
        name="in_proj",
    )(x, mod3, g_attn.reshape(1, D), w, wvt, cq, sq, ckv, skv)


def _compress_kernel(kr_ref, vr_ref, pk_ref, w1k_ref, w2k_ref, pv_ref, w1v_ref, w2vt_ref, ko_ref, vo_ref):
    def hidden(r_ref, p_ref, w1_ref):
        r = r_ref[0]
        half = r.shape[1]
        w1 = w1_ref[...]
        top = _dot(r, w1[:half])
        bot = _dot(r, w1[half:])
        bias = _dot_f32(jnp.broadcast_to(p_ref[...], (SUBLANES, 2 * half)), w1)[0:1]
        n = top.shape[0]
        return _silu(top + pltpu.roll(bot, n - 1, 0) + bias)

    ko_ref[0] = _dot(hidden(kr_ref, pk_ref, w1k_ref), w2k_ref[...]).astype(BF16)
    vo_ref[0] = _dot_nt(w2vt_ref[...], hidden(vr_ref, pv_ref, w1v_ref)).astype(BF16)


def _compress(kc, vc, pos_k, w1_k, w2_k, pos_v, w1_v, w2_v):
    B, S, dh = kc.shape
    ng = S // NSA_CMP_STRIDE
    width = NSA_CMP_STRIDE * dh
    kr = kc.reshape(B, ng, width)
    vr = vc.reshape(B, ng, width)
    full = lambda a: pl.BlockSpec(a.shape, lambda b: (0,) * a.ndim)
    pk = pos_k.reshape(1, NSA_CMP_LEN * dh)
    pv = pos_v.reshape(1, NSA_CMP_LEN * dh)
    args = (kr, vr, pk, w1_k, w2_k, pv, w1_v, w2_v.T)
    return pl.pallas_call(
        _compress_kernel,
        grid=(B,),
        in_specs=[pl.BlockSpec((1, ng, width), lambda b: (b, 0, 0))] * 2 + [full(a) for a in args[2:]],
        out_specs=[pl.BlockSpec((1, ng, dh), lambda b: (b, 0, 0)), pl.BlockSpec((1, dh, ng), lambda b: (b, 0, 0))],
        out_shape=[jax.ShapeDtypeStruct((B, ng, dh), BF16), jax.ShapeDtypeStruct((B, dh, ng), BF16)],
        compiler_params=_cparams(("parallel",)),
        name="nsa_compress",
    )(*args)


def _nsa_cmp_kernel(q_ref, kc_ref, vct_ref, o_ref, selt_ref, *, tq, n_sel):
    qi = pl.program_id(1)
    kc = kc_ref[0]
    vct = vct_ref[0]
    ng = kc.shape[0]
    nidx = lax.broadcasted_iota(I32, (ng, tq), 0)
    t = qi * tq + lax.broadcasted_iota(I32, (ng, tq), 1)
    cmask = (nidx * NSA_CMP_STRIDE + NSA_CMP_LEN - 1 <= t) & (nidx < ng - 1)
    psum = jnp.zeros((ng, tq), F32)
    outs = []
    scores = [_dot_nt(kc, q_ref[0, :, HEAD_DIM * h:HEAD_DIM * (h + 1)]) for h in range(NSA_HEADS)]
    for h in range(NSA_HEADS):
        s = jnp.where(cmask, scores[h], NEG)
        m = jnp.max(s, axis=0, keepdims=True)
        e = jnp.where(cmask, jnp.exp2(s - m), 0.0)
        d = jnp.sum(e, axis=0, keepdims=True)
        p = e / jnp.where(d > 0.0, d, 1.0)
        outs.append(_dot(vct, p).T)
        psum = psum + p
    o_ref[0] = jnp.concatenate(outs, axis=1)
    ss = lax.broadcasted_iota(I32, (n_sel, ng), 0) * NSA_SEL_LEN
    cs = lax.broadcasted_iota(I32, (n_sel, ng), 1) * NSA_CMP_STRIDE
    overlap_t = ((cs < ss + NSA_SEL_LEN) & (cs + NSA_CMP_LEN > ss)).astype(F32)
    imp = _dot_f32(overlap_t, psum)
    j = lax.broadcasted_iota(I32, (n_sel, tq), 0)
    cur = (qi * tq + lax.broadcasted_iota(I32, (n_sel, tq), 1)) // NSA_SEL_LEN
    valid = j <= cur
    forced = valid & ((j == 0) | (j == cur) | (j == cur - 1))
    score = jnp.where(forced, jnp.inf, jnp.where(valid, imp, -jnp.inf))
    sel = _topk_rows(score, min(NSA_SEL_TOPK, n_sel)) & valid
    selt_ref[0] = sel.astype(F32)


def _nsa_cmp(qa, kcmp, vcmp_t):
    B, S, W = qa.shape
    tq = SEQ_TILE
    ng = kcmp.shape[1]
    n_sel = S // NSA_SEL_LEN
    return pl.pallas_call(
        functools.partial(_nsa_cmp_kernel, tq=tq, n_sel=n_sel),
        grid=(B, S // tq),
        in_specs=[pl.BlockSpec((1, tq, W), lambda b, i: (b, i, 0)),
                  pl.BlockSpec((1, ng, HEAD_DIM), lambda b, i: (b, 0, 0)),
                  pl.BlockSpec((1, HEAD_DIM, ng), lambda b, i: (b, 0, 0))],
        out_specs=[pl.BlockSpec((1, tq, W), lambda b, i: (b, i, 0)),
                   pl.BlockSpec((1, n_sel, tq), lambda b, i: (b, 0, i))],
        out_shape=[jax.ShapeDtypeStruct((B, S, W), F32), jax.ShapeDtypeStruct((B, n_sel, S), F32)],
        compiler_params=_cparams(("parallel", "parallel")),
        name="nsa_cmp_select",
    )(qa, kcmp, vcmp_t)


def _nsa_flash_kernel(*refs, mode, tq):
    if mode == "slc":
        q_ref, k_ref, vt_ref, selt_ref, o_ref, qs_ref, m_ref, acc_ref = refs
    else:
        q_ref, k_ref, vt_ref, o_ref, qs_ref, m_ref, acc_ref = refs
    H = NSA_HEADS
    tk = tq
    qi = pl.program_id(1)
    for h in range(H):
        qs_ref[h * tq:(h + 1) * tq, :] = q_ref[0, :, HEAD_DIM * h:HEAD_DIM * (h + 1)]
    m_ref[...] = jnp.full(m_ref.shape, NEG, F32)
    acc_ref[...] = jnp.zeros(acc_ref.shape, F32)
    krow = lax.broadcasted_iota(I32, (tk, tq), 0)
    t_pos = qi * tq + lax.broadcasted_iota(I32, (tk, tq), 1)

    def step(kt):
        k0 = pl.multiple_of(kt * tk, tk)
        k = k_ref[0, pl.ds(k0, tk), :]
        vt = vt_ref[0, :, pl.ds(k0, tk)]
        kpos = k0 + krow
        if mode == "slc":
            n_sel = selt_ref.shape[1]
            kblk = (k0 + lax.broadcasted_iota(I32, (tk, n_sel), 0)) // NSA_SEL_LEN
            expand = (kblk == lax.broadcasted_iota(I32, (tk, n_sel), 1)).astype(BF16)
            mask = (_dot(expand, selt_ref[0]) > 0.5) & (kpos <= t_pos)
        else:
            mask = (kpos <= t_pos) & (kpos > t_pos - NSA_WINDOW)
        group = H // 4
        s_grp = [_dot_nt(k, qs_ref[g * group * tq:(g + 1) * group * tq, :]) for g in range(H // group)]
        for h in range(H):
            hs = slice(h * tq, (h + 1) * tq)
            gs = slice((h % group) * tq, (h % group + 1) * tq)
            s = jnp.where(mask, s_grp[h // group][:, gs], NEG)
            m_prev = m_ref[:, hs]
            m_new = jnp.maximum(m_prev, jnp.max(s, axis=0, keepdims=True))
            alpha = jnp.exp2(m_prev - m_new)
            p = jnp.exp2(s - m_new)
            acc_ref[:, hs] = alpha * acc_ref[:, hs] + _dot(vt, p)
            m_ref[:, hs] = m_new

    step(qi)
    lo = 0 if mode == "slc" else jnp.maximum(qi - NSA_WINDOW // tk, 0)

    def body(kt, carry):
        step(kt)
        return carry

    lax.fori_loop(lo, qi, body, 0)
    o_t = acc_ref[0:HEAD_DIM, :] / acc_ref[HEAD_DIM:HEAD_DIM + 1, :]
    o_ref[0] = jnp.concatenate([o_t[:, h * tq:(h + 1) * tq].T for h in range(H)], axis=1)


def _nsa_flash(mode, qa, k, v_t, sel_t=None):
    B, S, W = qa.shape
    tq = SEQ_TILE
    H = NSA_HEADS
    in_specs = [pl.BlockSpec((1, tq, W), lambda b, i: (b, i, 0)),
                pl.BlockSpec((1, S, HEAD_DIM), lambda b, i: (b, 0, 0)),
                pl.BlockSpec((1, VT_HEAD, S), lambda b, i: (b, 0, 0))]
    args = [qa, k, v_t]
    if mode == "slc":
        in_specs.append(pl.BlockSpec((1, sel_t.shape[1], tq), lambda b, i: (b, 0, i)))
        args.append(sel_t)
    return pl.pallas_call(
        functools.partial(_nsa_flash_kernel, mode=mode, tq=tq),
        grid=(B, S // tq),
        in_specs=in_specs,
        out_specs=pl.BlockSpec((1, tq, W), lambda b, i: (b, i, 0)),
        out_shape=jax.ShapeDtypeStruct((B, S, W), F32),
        scratch_shapes=[pltpu.VMEM((H * tq, HEAD_DIM), BF16),
                        pltpu.VMEM((1, H * tq), F32),
                        pltpu.VMEM((VT_HEAD, H * tq), F32)],
        compiler_params=_cparams(("parallel", "parallel")),
        name="nsa_flash_" + mode,
    )(*args)


def _moba_kernel(q_ref, k_ref, vt_ref, km_ref, o_ref, qs_ref, sel_ref, m_ref, acc_ref, *, nb):
    tq = MOBA_BLOCK
    H = MOBA_HEADS
    own = pl.program_id(1)
    own0 = pl.multiple_of(own * tq, tq)
    krow = lax.broadcasted_iota(I32, (tq, tq), 0)
    qcol = lax.broadcasted_iota(I32, (tq, tq), 1)
    blk = lax.broadcasted_iota(I32, (nb, tq), 0)
    heads = [slice(HEAD_DIM * h, HEAD_DIM * (h + 1)) for h in range(H)]
    vheads = [slice(VT_HEAD * h, VT_HEAD * (h + 1)) for h in range(H)]
    for h, sl in enumerate(heads):
        qs_ref[h] = q_ref[0, :, sl]
    gates = [_dot_nt_f32(km_ref[0, :, sl], qs_ref[h]) for h, sl in enumerate(heads)]
    scores = [_dot_nt(k_ref[0, pl.ds(own0, tq), sl], qs_ref[h]) for h, sl in enumerate(heads)]
    for h in range(H):
        gate = jnp.where(blk < own, gates[h], -jnp.inf)
        sel_ref[h] = (_topk_rows(gate, min(MOBA_TOPK, nb)) & (blk < own)).astype(F32)
        s = jnp.where(krow <= qcol, scores[h], NEG)
        m = jnp.max(s, axis=0, keepdims=True)
        m_ref[h] = m
        acc_ref[h] = _dot(vt_ref[0, vheads[h], pl.ds(own0, tq)], jnp.exp2(s - m))

    def body(n, carry):
        n0 = pl.multiple_of(n * tq, tq)
        scores = [_dot_nt(k_ref[0, pl.ds(n0, tq), sl], qs_ref[h]) for h, sl in enumerate(heads)]
        for h in range(H):
            chosen = sel_ref[h, pl.ds(n, 1), :] > 0.5
            s = scores[h]
            m_prev = m_ref[h]
            m_new = jnp.where(chosen, jnp.maximum(m_prev, jnp.max(s, axis=0, keepdims=True)), m_prev)
            alpha = jnp.exp2(m_prev - m_new)
            p = jnp.exp2(s - m_new)
            pv = _dot(vt_ref[0, vheads[h], pl.ds(n0, tq)], p)
            acc_ref[h] = alpha * acc_ref[h] + jnp.where(chosen, pv, 0.0)
            m_ref[h] = m_new
        return carry

    lax.fori_loop(0, own, body, 0)
    o_ref[0] = jnp.concatenate([(acc_ref[h, 0:HEAD_DIM] / acc_ref[h, HEAD_DIM:HEAD_DIM + 1]).T for h in range(H)],
                               axis=1)


def _moba(qb, kb, vb_t, kmean):
    B, S, W = qb.shape
    tq = MOBA_BLOCK
    nb = S // tq
    H = MOBA_HEADS
    return pl.pallas_call(
        functools.partial(_moba_kernel, nb=nb),
        grid=(B, nb),
        in_specs=[pl.BlockSpec((1, tq, W), lambda b, i: (b, i, 0)),
                  pl.BlockSpec((1, S, W), lambda b, i: (b, 0, 0)),
                  pl.BlockSpec((1, H * VT_HEAD, S), lambda b, i: (b, 0, 0)),
                  pl.BlockSpec((1, nb, W), lambda b, i: (b, 0, 0))],
        out_specs=pl.BlockSpec((1, tq, W), lambda b, i: (b, i, 0)),
        out_shape=jax.ShapeDtypeStruct((B, S, W), F32),
        scratch_shapes=[pltpu.VMEM((H, tq, HEAD_DIM), BF16),
                        pltpu.VMEM((H, nb, tq), F32),
                        pltpu.VMEM((H, 1, tq), F32),
                        pltpu.VMEM((H, VT_HEAD, tq), F32)],
        compiler_params=_cparams(("parallel", "parallel")),
        name="moba",
    )(qb, kb, vb_t, kmean)


def _post_attn_kernel(x_ref, mod_ref, oc_ref, os_ref, ow_ref, gt_ref, ob_ref, wo_ref, gf_ref, wrt_ref, rb_ref,
                      wsg_ref, wsu_ref, wsd_ref, h2_ref, pre_ref, eidx_ref, wts_ref, hot_ref, cnt_ref):
    first = (pl.program_id(0) == 0) & (pl.program_id(1) == 0)
    x = x_ref[0]
    tm = x.shape[0]
    gt1 = mod_ref[0, 2:3, :]
    sh2 = mod_ref[0, 3:4, :]
    sc2 = mod_ref[0, 4:5, :]
    gt2 = mod_ref[0, 5:6, :]
    gates = gt_ref[0]
    gi = lax.broadcasted_iota(I32, (LANES, 512), 0)
    hi = lax.broadcasted_iota(I32, (LANES, 512), 1) // HEAD_DIM
    o_a = jnp.zeros((tm, 512), F32)
    for jbr, ref in enumerate((oc_ref, os_ref, ow_ref)):
        expand = (gi == 3 * hi + jbr).astype(F32)
        o_a = o_a + _dot_f32(gates, expand) * ref[0]
    wo = wo_ref[...]
    attn = _dot(o_a, wo[:512]) + _dot(ob_ref[0], wo[512:])
    h1 = x + gt1 * attn
    h2 = (_rms(h1, gf_ref[...]) * (1.0 + sc2) + sh2).astype(BF16)
    half = h2.shape[1] // 2
    lo = lax.bitcast_convert_type(h2[:, :half].astype(F32), U32) >> 16
    hi = lax.bitcast_convert_type(h2[:, half:].astype(F32), U32) & jnp.uint32(0xFFFF0000)
    words = lo | hi
    for s in range(PACKED_SUBLANES):
        h2_ref[pl.ds(s, tm, stride=PACKED_SUBLANES), :] = words[:, LANES * s:LANES * (s + 1)]
    shared = _dot(_silu(_dot(h2, wsg_ref[...])) * _dot(h2, wsu_ref[...]), wsd_ref[...])
    pre = h1 + gt2 * shared
    for s in range(SUBLANES):
        pre_ref[pl.ds(s, tm, stride=SUBLANES), :] = pre[:, LANES * s:LANES * (s + 1)]
    scores = jax.nn.sigmoid(_dot_nt(wrt_ref[...], h2))
    biased = scores + rb_ref[...]
    E = N_EXPERTS
    per = E // N_GROUPS
    rowp = lax.broadcasted_iota(I32, (per, tm), 0)
    gs, vals = [], []
    for g in range(N_GROUPS):
        v = biased[per * g:per * (g + 1), :]
        m1 = jnp.max(v, axis=0, keepdims=True)
        i1 = jnp.min(jnp.where(v == m1, rowp, per), axis=0, keepdims=True)
        m2 = jnp.max(jnp.where(rowp == i1, -jnp.inf, v), axis=0, keepdims=True)
        gs.append(m1 + m2)
        vals.append(v)
    cands = []
    for g in range(N_GROUPS):
        rank = jnp.zeros((1, tm), I32)
        for i in range(N_GROUPS):
            beats = (gs[i] > gs[g]) | ((gs[i] == gs[g]) & (i < g))
            rank = rank + beats.astype(I32)
        cands.append(jnp.where(rank < TOPK_GROUPS, vals[g], -jnp.inf))
    cand = jnp.concatenate(cands, axis=0)
    row = lax.broadcasted_iota(I32, (E, tm), 0)
    krow = lax.broadcasted_iota(I32, (TOP_K, tm), 0)
    eidx = jnp.zeros((TOP_K, tm), I32)
    wsel = jnp.zeros((TOP_K, tm), F32)
    hot = jnp.zeros((E, tm), F32)
    wsum = jnp.zeros((1, tm), F32)
    for kk in range(TOP_K):
        m = jnp.max(cand, axis=0, keepdims=True)
        idx = jnp.min(jnp.where(cand == m, row, E), axis=0, keepdims=True)
        pick = row == idx
        w = jnp.sum(jnp.where(pick, scores, 0.0), axis=0, keepdims=True)
        cand = jnp.where(pick, -jnp.inf, cand)
        hot = hot + pick.astype(F32)
        eidx = jnp.where(krow == kk, idx, eidx)
        wsel = jnp.where(krow == kk, w, wsel)
        wsum = wsum + w
    eidx_ref[...] = eidx
    wts_ref[...] = wsel / wsum * ROUTED_SCALE
    hot_ref[...] = hot.astype(BF16)

    @pl.when(first)
    def _():
        cnt_ref[...] = jnp.zeros(cnt_ref.shape, F32)

    cnt_ref[...] += jnp.sum(hot, axis=1, keepdims=True)


def _post_attn(x, mod3, o_cmp, o_slc, o_win, gates, o_b, w_out, g_ffn, w_router, router_bias, w_sg, w_su, w_sd):
    B, S, D = x.shape
    tm = SEQ_TILE
    nt = S // tm
    T = B * S
    E = N_EXPERTS
    tok = lambda w_: pl.BlockSpec((1, tm, w_), lambda b, i: (b, i, 0))
    full = lambda a: pl.BlockSpec(a.shape, lambda b, i: (0,) * a.ndim)
    rows = lambda r_: pl.BlockSpec((tm * r_, LANES), lambda b, i: (b * nt + i, 0))
    cols = lambda r_: pl.BlockSpec((r_, tm), lambda b, i: (0, b * nt + i))
    consts = (w_out.astype(BF16), g_ffn.reshape(1, D), w_router.T.astype(BF16), router_bias.reshape(E, 1),
              w_sg.astype(BF16), w_su.astype(BF16), w_sd.astype(BF16))
    return pl.pallas_call(
        _post_attn_kernel,
        grid=(B, nt),
        in_specs=[tok(D), pl.BlockSpec((1, 6, D), lambda b, i: (b, 0, 0)),
                  tok(512), tok(512), tok(512), tok(LANES), tok(512)] + [full(a) for a in consts],
        out_specs=[rows(PACKED_SUBLANES), rows(SUBLANES), cols(TOP_K), cols(TOP_K), cols(E),
                   pl.BlockSpec((E, 1), lambda b, i: (0, 0))],
        out_shape=[jax.ShapeDtypeStruct((T * PACKED_SUBLANES, LANES), U32),
                   jax.ShapeDtypeStruct((T * SUBLANES, LANES), F32),
                   jax.ShapeDtypeStruct((TOP_K, T), I32), jax.ShapeDtypeStruct((TOP_K, T), F32),
                   jax.ShapeDtypeStruct((E, T), BF16), jax.ShapeDtypeStruct((E, 1), F32)],
        compiler_params=_cparams(("arbitrary", "arbitrary")),
        name="post_attn_router",
    )(x, mod3, o_cmp, o_slc, o_win, gates, o_b, *consts)


def _moe_pos_kernel(hot_ref, eidx_ref, start_ref, dest_ref, carry_ref):
    @pl.when(pl.program_id(0) == 0)
    def _():
        carry_ref[...] = start_ref[...]

    hot = hot_ref[...]
    E, tp = hot.shape
    r = lax.broadcasted_iota(I32, (tp, tp), 0)
    c = lax.broadcasted_iota(I32, (tp, tp), 1)
    earlier = (r < c).astype(BF16)
    pos = carry_ref[...] + _dot(hot, earlier)
    row = lax.broadcasted_iota(I32, (E, tp), 0)
    krow = lax.broadcasted_iota(I32, (TOP_K, tp), 0)
    eidx = eidx_ref[...]
    dest = jnp.zeros((TOP_K, tp), I32)
    for kk in range(TOP_K):
        d = jnp.sum(jnp.where(row == eidx[kk:kk + 1, :], pos, 0.0), axis=0, keepdims=True)
        dest = jnp.where(krow == kk, d.astype(I32), dest)
    dest_ref[...] = dest
    carry_ref[...] += jnp.sum(hot.astype(F32), axis=1, keepdims=True)


def _moe_pos(hot, eidx, starts):
    E, T = hot.shape
    tp = 512 if T % 512 == 0 else SEQ_TILE
    return pl.pallas_call(
        _moe_pos_kernel,
        grid=(T // tp,),
        in_specs=[pl.BlockSpec((E, tp), lambda i: (0, i)),
                  pl.BlockSpec((TOP_K, tp), lambda i: (0, i)),
                  pl.BlockSpec((E, 1), lambda i: (0, 0))],
        out_specs=pl.BlockSpec((TOP_K, tp), lambda i: (0, i)),
        out_shape=jax.ShapeDtypeStruct((TOP_K, T), I32),
        scratch_shapes=[pltpu.VMEM((E, 1), F32)],
        compiler_params=_cparams(("arbitrary",)),
        name="moe_pos",
    )(hot, eidx, starts)


def _row(ref, start, n=SUBLANES):
    return ref.at[pl.ds(pl.multiple_of(start, n), n)]


def _moe_scatter_kernel(dest_ref, last_ref, h_ref, xs_ref, zbuf, sem, zsem):
    n = PACKED_SUBLANES
    ts = h_ref.shape[0] // n

    @pl.when(pl.program_id(0) == 0)
    def _():
        zbuf[...] = jnp.zeros(zbuf.shape, U32)

        def zero_copy(e):
            return pltpu.make_async_copy(zbuf, xs_ref.at[pl.ds(pl.multiple_of(last_ref[e], EXPERT_ROWS),
                                                               EXPERT_ROWS)], zsem)

        def zstart(e, carry):
            @pl.when(last_ref[e] >= 0)
            def _():
                zero_copy(e).start()
            return carry

        def zwait(e, carry):
            @pl.when(last_ref[e] >= 0)
            def _():
                zero_copy(e).wait()
            return carry

        lax.fori_loop(0, N_EXPERTS, zstart, 0)
        lax.fori_loop(0, N_EXPERTS, zwait, 0)

    def copy(r, kk):
        return pltpu.make_async_copy(_row(h_ref, r * n, n), xs_ref.at[dest_ref[r * TOP_K + kk]], sem)

    def issue(r, carry):
        for kk in range(TOP_K):
            copy(r, kk).start(priority=kk % 2)
        return carry

    def drain(r, carry):
        for kk in range(TOP_K):
            copy(r, kk).wait()
        return carry

    lax.fori_loop(0, ts, issue, 0)
    lax.fori_loop(0, ts, drain, 0)


def _moe_scatter(h2p, dest, last_block_row, n_rows):
    n = PACKED_SUBLANES
    T = h2p.shape[0] // n
    ts = MOE_TILE
    return pl.pallas_call(
        _moe_scatter_kernel,
        grid=(T // ts,),
        in_specs=[pl.BlockSpec((ts * TOP_K,), lambda i: (i,), memory_space=pltpu.SMEM),
                  pl.BlockSpec((N_EXPERTS,), lambda i: (0,), memory_space=pltpu.SMEM),
                  pl.BlockSpec((ts * n, LANES), lambda i: (i, 0))],
        out_specs=pl.BlockSpec(memory_space=pl.ANY),
        out_shape=jax.ShapeDtypeStruct((n_rows, n, LANES), U32),
        scratch_shapes=[pltpu.VMEM((EXPERT_ROWS, n, LANES), U32), pltpu.SemaphoreType.DMA(()),
                        pltpu.SemaphoreType.DMA(())],
        compiler_params=_cparams(("arbitrary",)),
        name="moe_scatter",
    )(dest, last_block_row, h2p)


def _moe_experts_kernel(be_ref, bf_ref, nx_ref, wp_ref, nu_ref, xs_ref, wg_ref, wu_ref, wd_ref, ys_ref,
                        wg_s, wu_s, wd_s, wg_f, wu_f, wd_f, xbuf, ybuf, sem_in, sem_out, sem_w):
    i = pl.program_id(0)
    n_used = nu_ref[0]
    rows = EXPERT_ROWS
    slot = i % 2
    islot = i % IN_SLOTS

    def w_copies(e, sl):
        return [pltpu.make_async_copy(src.at[e], dst.at[sl], sem_w.at[sl])
                for src, dst in ((wg_ref, wg_f), (wu_ref, wu_f), (wd_ref, wd_f))]

    def in_copies(blk):
        r0 = pl.multiple_of(blk * rows, rows)
        sl = blk % IN_SLOTS
        return [pltpu.make_async_copy(xs_ref.at[pl.ds(r0, rows), s], xbuf.at[sl, s], sem_in.at[sl])
                for s in range(PACKED_SUBLANES)]

    def out_copies(blk, sl):
        r0 = pl.multiple_of(blk * rows, rows)
        return [pltpu.make_async_copy(ybuf.at[sl, s], ys_ref.at[pl.ds(r0, rows), s], sem_out.at[sl])
                for s in range(SUBLANES)]

    @pl.when(i == 0)
    def _():
        for cp in w_copies(be_ref[0], 0):
            cp.start()
        for ahead in range(IN_SLOTS - 1):
            @pl.when(ahead < n_used)
            def _():
                for cp in in_copies(ahead):
                    cp.start()

    @pl.when(i + IN_SLOTS - 1 < n_used)
    def _():
        for cp in in_copies(i + IN_SLOTS - 1):
            cp.start()

    @pl.when((bf_ref[i] == 1) & (i < n_used))
    def _():
        wslot = wp_ref[i]
        for cp in w_copies(be_ref[i], wslot):
            cp.wait()
        wg_s[...] = wg_f[wslot].astype(BF16)
        wu_s[...] = wu_f[wslot].astype(BF16)
        wd_s[...] = wd_f[wslot].astype(BF16)

        @pl.when(nx_ref[i] >= 0)
        def _():
            for cp in w_copies(nx_ref[i], 1 - wslot):
                cp.start()

    @pl.when(i < n_used)
    def _():
        for cp in in_copies(i):
            cp.wait()

        @pl.when(i >= 2)
        def _():
            for cp in out_copies(i - 2, slot):
                cp.wait()

        words = [xbuf[islot, s] for s in range(PACKED_SUBLANES)]
        lo = [lax.bitcast_convert_type(w << 16, F32).astype(BF16) for w in words]
        hi = [lax.bitcast_convert_type(w & jnp.uint32(0xFFFF0000), F32).astype(BF16) for w in words]
        x = jnp.concatenate(lo + hi, axis=1)
        a = _silu(_dot(x, wg_s[...])) * _dot(x, wu_s[...])
        y = _dot(a, wd_s[...])
        for s in range(SUBLANES):
            ybuf[slot, s] = y[:, LANES * s:LANES * (s + 1)]
        for cp in out_copies(i, slot):
            cp.start()

        @pl.when(i == n_used - 1)
        def _():
            for cp in out_copies(i, slot):
                cp.wait()

            @pl.when(i >= 1)
            def _():
                for cp in out_copies(i - 1, 1 - slot):
                    cp.wait()


def _moe_experts(xs, blk_exp, blk_first, blk_next, blk_wslot, n_used, w_eg, w_eu, w_ed):
    n_rows = xs.shape[0]
    E, D, De = w_eg.shape
    n_blk = n_rows // EXPERT_ROWS
    grid_spec = pltpu.PrefetchScalarGridSpec(
        num_scalar_prefetch=5,
        grid=(n_blk,),
        in_specs=[pl.BlockSpec(memory_space=pl.ANY)] * 4,
        out_specs=pl.BlockSpec(memory_space=pl.ANY),
        scratch_shapes=[pltpu.VMEM((D, De), BF16), pltpu.VMEM((D, De), BF16), pltpu.VMEM((De, D), BF16),
                        pltpu.VMEM((2, D, De), F32), pltpu.VMEM((2, D, De), F32), pltpu.VMEM((2, De, D), F32),
                        pltpu.VMEM((IN_SLOTS, PACKED_SUBLANES, EXPERT_ROWS, LANES), U32),
                        pltpu.VMEM((2, SUBLANES, EXPERT_ROWS, LANES), F32),
                        pltpu.SemaphoreType.DMA((IN_SLOTS,)), pltpu.SemaphoreType.DMA((2,)),
                        pltpu.SemaphoreType.DMA((2,))],
    )
    return pl.pallas_call(
        _moe_experts_kernel,
        grid_spec=grid_spec,
        out_shape=jax.ShapeDtypeStruct((n_rows, SUBLANES, LANES), F32),
        compiler_params=_cparams(("arbitrary",)),
        name="moe_experts",
    )(blk_exp, blk_first, blk_next, blk_wslot, n_used, xs, w_eg, w_eu, w_ed)


def _moe_combine_kernel(dcur_ref, dnxt_ref, w_ref, pre_ref, gt2_ref, gfin_ref, ys_ref, o_ref, buf, hbuf, sem):
    i = pl.program_id(0)
    n = pl.num_programs(0)
    tc = pre_ref.shape[0] // SUBLANES
    slot = i % 2

    def copy(d_ref, sl, j):
        return pltpu.make_async_copy(ys_ref.at[d_ref[j]], _row(buf.at[sl], j * SUBLANES), sem.at[sl])

    def issue(d_ref, sl):
        def body(r2, carry):
            for u in range(2 * TOP_K):
                copy(d_ref, sl, r2 * (2 * TOP_K) + u).start(priority=u % 2)
            return carry
        lax.fori_loop(0, tc // 2, body, 0)

    @pl.when(i == 0)
    def _():
        issue(dcur_ref, 0)

    @pl.when(i + 1 < n)
    def _():
        issue(dnxt_ref, 1 - slot)

    def drain(r, carry):
        for kk in range(TOP_K):
            copy(dcur_ref, slot, r * TOP_K + kk).wait()
        return carry

    lax.fori_loop(0, tc, drain, 0)
    cur = buf.at[slot]
    gt2 = gt2_ref[...]
    unroll = 4

    def tokens(g, carry):
        for u in range(unroll):
            r = g * unroll + u
            routed = jnp.zeros((SUBLANES, LANES), F32)
            for kk in range(TOP_K):
                j = r * TOP_K + kk
                routed = routed + w_ref[j] * _row(cur, j * SUBLANES)[...]
            _row(hbuf, r * SUBLANES)[...] = _row(pre_ref, r * SUBLANES)[...] + gt2 * routed
        return carry

    lax.fori_loop(0, tc // unroll, tokens, 0)
    h = jnp.concatenate([hbuf[pl.ds(s, tc, stride=SUBLANES), :] for s in range(SUBLANES)], axis=1)
    o_ref[...] = _rms(h, gfin_ref[...])


def _moe_combine(ys, dest8, w_flat, pre2d, gt2_2d, gfin, S):
    T = pre2d.shape[0] // SUBLANES
    tc = MOE_TILE
    per_b = S // tc
    n = T // tc
    return pl.pallas_call(
        _moe_combine_kernel,
        grid=(n,),
        in_specs=[pl.BlockSpec((tc * TOP_K,), lambda i: (i,), memory_space=pltpu.SMEM),
                  pl.BlockSpec((tc * TOP_K,), lambda i: (jnp.minimum(i + 1, n - 1),), memory_space=pltpu.SMEM),
                  pl.BlockSpec((tc * TOP_K,), lambda i: (i,), memory_space=pltpu.SMEM),
                  pl.BlockSpec((tc * SUBLANES, LANES), lambda i: (i, 0)),
                  pl.BlockSpec((SUBLANES, LANES), lambda i: (i // per_b, 0)),
                  pl.BlockSpec((1, SUBLANES * LANES), lambda i: (0, 0)),
                  pl.BlockSpec(memory_space=pl.ANY)],
        out_specs=pl.BlockSpec((tc, SUBLANES * LANES), lambda i: (i, 0)),
        out_shape=jax.ShapeDtypeStruct((T, SUBLANES * LANES), F32),
        scratch_shapes=[pltpu.VMEM((2, tc * TOP_K * SUBLANES, LANES), F32),
                        pltpu.VMEM((tc * SUBLANES, LANES), F32),
                        pltpu.SemaphoreType.DMA((2,))],
        compiler_params=_cparams(("arbitrary",)),
        name="moe_combine",
    )(dest8, dest8, w_flat, pre2d, gt2_2d, gfin, ys)


def _layer(h_res, mod, g_attn, w_in, cmp_pos_k, cmp_w1_k, cmp_w2_k, cmp_pos_v, cmp_w1_v, cmp_w2_v,
           w_out, g_ffn, w_router, router_bias, w_eg, w_eu, w_ed, w_sg, w_su, w_sd):
    B, S, D = h_res.shape
    T = B * S
    mod3 = mod.reshape(B, 6, D)
    (qa, kc, vc, ks, kw, gates, qb, kb, kmean, vb_t, vs_t, vw_t) = _in_proj(h_res, mod3, g_attn, w_in)
    kcmp, vcmp_t = _compress(kc, vc, cmp_pos_k, cmp_w1_k, cmp_w2_k, cmp_pos_v, cmp_w1_v, cmp_w2_v)
    o_cmp, sel_t = _nsa_cmp(qa, kcmp, vcmp_t)
    o_slc = _nsa_flash("slc", qa, ks, vs_t, sel_t)
    o_win = _nsa_flash("win", qa, kw, vw_t)
    o_b = _moba(qb, kb, vb_t, kmean.reshape(B, S // MOBA_BLOCK, 512))
    h2p, pre2d, eidx, wts, hot, counts = _post_attn(h_res, mod3, o_cmp, o_slc, o_win, gates, o_b, w_out, g_ffn,
                                                    w_router, router_bias, w_sg, w_su, w_sd)
    counts = counts.reshape(N_EXPERTS).astype(I32)
    padded = (counts + EXPERT_ROWS - 1) // EXPERT_ROWS * EXPERT_ROWS
    ends = jnp.cumsum(padded)
    starts = ends - padded
    n_blk = -(-(T * TOP_K) // EXPERT_ROWS) + N_EXPERTS
    n_rows = n_blk * EXPERT_ROWS
    blk_row = jnp.arange(n_blk, dtype=I32) * EXPERT_ROWS
    blk_exp = jnp.minimum(jnp.sum((ends[None, :] <= blk_row[:, None]).astype(I32), axis=1), N_EXPERTS - 1)
    blk_first = jnp.concatenate([jnp.ones((1,), I32), (blk_exp[1:] != blk_exp[:-1]).astype(I32)])
    n_used = (ends[-1:] // EXPERT_ROWS).astype(I32)
    blk_idx = jnp.arange(n_blk, dtype=I32)
    first_used = (blk_first == 1) & (blk_idx < n_used[0])
    first_pos = jnp.where(first_used, blk_idx, n_blk)
    next_pos = jnp.concatenate([lax.cummin(first_pos, reverse=True)[1:], jnp.full((1,), n_blk, I32)])
    blk_next = jnp.where(next_pos < n_blk, blk_exp[jnp.minimum(next_pos, n_blk - 1)], -1).astype(I32)
    blk_wslot = ((jnp.cumsum(first_used.astype(I32)) - 1) % 2).astype(I32)
    dest = _moe_pos(hot, eidx, starts.astype(F32).reshape(N_EXPERTS, 1))
    dest_flat = dest.T.reshape(T * TOP_K)
    w_flat = wts.T.reshape(T * TOP_K)
    last_block_row = jnp.where(counts > 0, ends - EXPERT_ROWS, -1).astype(I32)
    xs = _moe_scatter(h2p, dest_flat, last_block_row, n_rows)
    ys = _moe_experts(xs, blk_exp, blk_first, blk_next, blk_wslot, n_used, w_eg, w_eu, w_ed)
    gt2_2d = mod3[:, 5, :].reshape(B * SUBLANES, LANES)
    return ys, dest_flat, w_flat, pre2d, gt2_2d


def kernel(x, c, w_ada, b_ada, g_attn, w_in, cmp_pos_k, cmp_w1_k, cmp_w2_k, cmp_pos_v, cmp_w1_v, cmp_w2_v, w_out, g_ffn, w_router, router_bias, w_exp_gate, w_exp_up, w_exp_down, w_sh_gate, w_sh_up, w_sh_down, g_final):
    B, S, D = x.shape
    depth = w_ada.shape[0]
    assert depth == 1, "the final RMSNorm is fused into the single layer's MoE combine"
    assert D == SUBLANES * LANES and S % SEQ_TILE == 0
    l = 0
    mod = _ada_mod(c.astype(F32), w_ada[l], b_ada[l])
    ys, dest8, w_flat, pre2d, gt2_2d = _layer(
        x.astype(F32), mod, g_attn[l], w_in[l], cmp_pos_k[l], cmp_w1_k[l], cmp_w2_k[l], cmp_pos_v[l], cmp_w1_v[l],
        cmp_w2_v[l], w_out[l], g_ffn[l], w_router[l], router_bias[l], w_exp_gate[l], w_exp_up[l], w_exp_down[l],
        w_sh_gate[l], w_sh_up[l], w_sh_down[l])
    out = _moe_combine(ys, dest8, w_flat, pre2d, gt2_2d, g_final.reshape(1, D).astype(F32), S)
    return out.reshape(B, S, D).astype(x.dtype)
```

```python
import functools

import jax
import jax.numpy as jnp
import numpy as np
from jax import lax
from jax.experimental import pallas as pl
from jax.experimental.pallas import tpu as pltpu

F32 = jnp.float32
BF16 = jnp.bfloat16
I32 = jnp.int32
U32 = jnp.uint32

HEAD_DIM = 64
HALF = HEAD_DIM // 2
NSA_HEADS = 8
NSA_CMP_LEN = 32
NSA_CMP_STRIDE = 16
NSA_SEL_LEN = 64
NSA_SEL_TOPK = 16
NSA_WINDOW = 512
MOBA_HEADS = 8
MOBA_BLOCK = 256
MOBA_TOPK = 3
N_EXPERTS = 256
TOP_K = 8
N_GROUPS = 8
TOPK_GROUPS = 4
ROUTED_SCALE = 2.5
ROPE_THETA = 10000.0
EPS = 1e-6
NEG = -1e30
LOG2_E = 1.4426950408889634
ONES_ROWS = 16
VT_HEAD = HEAD_DIM + ONES_ROWS

LANES = 128
SUBLANES = 8
PACKED_SUBLANES = SUBLANES // 2
VMEM_LIMIT = 56 * 1024 * 1024
EXPERT_ROWS = 256
SEQ_TILE = 256
MOE_TILE = 128
IN_SLOTS = 4


def _cparams(sem):
    return pltpu.CompilerParams(dimension_semantics=sem, vmem_limit_bytes=VMEM_LIMIT)


def _dot(a, b):
    return jnp.dot(a.astype(BF16), b.astype(BF16), preferred_element_type=F32)


def _dot_nt(a, b):
    return lax.dot_general(a.astype(BF16), b.astype(BF16), (((1,), (1,)), ((), ())),
                           preferred_element_type=F32)


def _dot_f32(a, b):
    return jnp.dot(a.astype(F32), b.astype(F32), preferred_element_type=F32)


def _dot_nt_f32(a, b):
    return lax.dot_general(a.astype(F32), b.astype(F32), (((1,), (1,)), ((), ())),
                           preferred_element_type=F32)


def _silu(x):
    return x * jax.nn.sigmoid(x)


def _rms(x, g):
    return x * lax.rsqrt(jnp.mean(x * x, axis=-1, keepdims=True) + EPS) * g


def _topk_rows(score, k):
    n = score.shape[0]
    idx = lax.broadcasted_iota(I32, score.shape, 0)
    rank = jnp.zeros(score.shape, I32)
    for i in range(n):
        row = score[i:i + 1, :]
        beats = (row > score) | ((row == score) & (idx > i))
        rank = rank + beats.astype(I32)
    return rank < k


def _ada_kernel(c_ref, w_ref, b_ref, o_ref):
    o_ref[...] = _dot_f32(_silu(c_ref[...]), w_ref[...]) + b_ref[...]


def _ada_mod(c, w, b):
    B, D = c.shape
    N = w.shape[1]
    tn = 1536 if N % 1536 == 0 else N
    return pl.pallas_call(
        _ada_kernel,
        grid=(N // tn,),
        in_specs=[pl.BlockSpec((B, D), lambda j: (0, 0)),
                  pl.BlockSpec((D, tn), lambda j: (0, j)),
                  pl.BlockSpec((1, tn), lambda j: (0, j))],
        out_specs=pl.BlockSpec((B, tn), lambda j: (0, j)),
        out_shape=jax.ShapeDtypeStruct((B, N), F32),
        compiler_params=_cparams(("arbitrary",)),
        name="ada_mod",
    )(c, w, b.reshape(1, N))


_OFF_QA, _OFF_KV, _OFF_G, _OFF_QB, _OFF_KB, _W_COLS = 0, 512, 896, 1024, 1536, 2048
_VT_ROWS = 640


def _swap_halves(t):
    n = t.shape[-1]
    lane = lax.broadcasted_iota(I32, t.shape, t.ndim - 1)
    left = pltpu.roll(t, n - HALF, t.ndim - 1)
    right = pltpu.roll(t, HALF, t.ndim - 1)
    return jnp.where((lane % HEAD_DIM) < HALF, left, right)


def _in_proj_kernel(x_ref, mod_ref, g_ref, w_ref, wvt_ref, cq_ref, sq_ref, ckv_ref, skv_ref,
                    qa_ref, kc_ref, vc_ref, ks_ref, kw_ref, gt_ref, qb_ref, kb_ref, km_ref,
                    vbt_ref, vst_ref, vwt_ref):
    x = x_ref[0]
    sh1 = mod_ref[0, 0:1, :]
    sc1 = mod_ref[0, 1:2, :]
    h = (_rms(x, g_ref[...]) * (1.0 + sc1) + sh1).astype(BF16)
    proj = _dot(h, w_ref[...])
    cq, sq = cq_ref[...], sq_ref[...]
    scale = HEAD_DIM ** -0.5 * LOG2_E

    def rope(t, c, s):
        return t * c + _swap_halves(t) * s

    qa = rope(proj[:, _OFF_QA:_OFF_QA + 512], cq, sq) * scale
    qa_ref[0] = qa.astype(BF16)
    kv = rope(proj[:, _OFF_KV:_OFF_KV + 384], ckv_ref[...], skv_ref[...])
    for j, ref in ((0, kc_ref), (1, vc_ref), (2, ks_ref), (4, kw_ref)):
        ref[0] = kv[:, HEAD_DIM * j:HEAD_DIM * (j + 1)].astype(BF16)
    gt_ref[0] = jax.nn.sigmoid(proj[:, _OFF_G:_OFF_G + LANES])
    qb = rope(proj[:, _OFF_QB:_OFF_QB + 512], cq, sq) * scale
    qb_ref[0] = qb.astype(BF16)
    kb = rope(proj[:, _OFF_KB:_OFF_KB + 512], cq, sq)
    kb_ref[0] = kb.astype(BF16)
    km_ref[0, 0] = jnp.mean(kb, axis=0, keepdims=True)
    vt = _dot_nt(wvt_ref[...], h)
    ones = jnp.ones((ONES_ROWS, vt.shape[1]), BF16)
    for h in range(MOBA_HEADS):
        vbt_ref[0, VT_HEAD * h:VT_HEAD * h + HEAD_DIM] = vt[HEAD_DIM * h:HEAD_DIM * (h + 1)].astype(BF16)
        vbt_ref[0, VT_HEAD * h + HEAD_DIM:VT_HEAD * (h + 1)] = ones
    for ref, r0 in ((vst_ref, 512), (vwt_ref, 576)):
        ref[0, 0:HEAD_DIM] = vt[r0:r0 + HEAD_DIM].astype(BF16)
        ref[0, HEAD_DIM:VT_HEAD] = ones


def _rope_tables(S):
    inv = ROPE_THETA ** (-jnp.arange(0, HEAD_DIM, 2, dtype=F32) / HEAD_DIM)
    ang = jnp.arange(S, dtype=F32)[:, None] * inv[None, :]
    cos, sin = jnp.cos(ang), jnp.sin(ang)
    c_head = jnp.concatenate([cos, cos], axis=-1)
    s_head = jnp.concatenate([-sin, sin], axis=-1)
    cq = jnp.tile(c_head, (1, 8))
    sq = jnp.tile(s_head, (1, 8))
    one, zero = jnp.ones_like(c_head), jnp.zeros_like(s_head)
    ckv = jnp.concatenate([c_head, one, c_head, one, c_head, one], axis=-1)
    skv = jnp.concatenate([s_head, zero, s_head, zero, s_head, zero], axis=-1)
    return cq, sq, ckv, skv


def _in_proj(x, mod3, g_attn, w_in):
    B, S, D = x.shape
    tm = SEQ_TILE
    nb = S // tm
    offs = np.cumsum((512, 64, 64, 64, 64, 64, 64, 24, 512, 512))
    cols = jnp.split(w_in, offs.tolist(), axis=1)
    g_w = jnp.pad(cols[7], ((0, 0), (0, LANES - cols[7].shape[1])))
    w = jnp.concatenate([cols[0]] + cols[1:7] + [g_w, cols[8], cols[9]], axis=1).astype(BF16)
    wvt = jnp.concatenate([cols[10], cols[4], cols[6]], axis=1).T.astype(BF16)
    cq, sq, ckv, skv = _rope_tables(S)
    tok = lambda w_: pl.BlockSpec((1, tm, w_), lambda b, i: (b, i, 0))
    tokt = lambda r_: pl.BlockSpec((1, r_, tm), lambda b, i: (b, 0, i))
    tab = lambda w_: pl.BlockSpec((tm, w_), lambda b, i: (i, 0))
    sds = lambda w_, dt: jax.ShapeDtypeStruct((B, S, w_), dt)
    sdst = lambda r_: jax.ShapeDtypeStruct((B, r_, S), BF16)
    return pl.pallas_call(
        _in_proj_kernel,
        grid=(B, nb),
        in_specs=[tok(D),
                  pl.BlockSpec((1, 6, D), lambda b, i: (b, 0, 0)),
                  pl.BlockSpec((1, D), lambda b, i: (0, 0)),
                  pl.BlockSpec((D, _W_COLS), lambda b, i: (0, 0)),
                  pl.BlockSpec((_VT_ROWS, D), lambda b, i: (0, 0)),
                  tab(512), tab(512), tab(384), tab(384)],
        out_specs=[tok(512)] + [tok(HEAD_DIM)] * 4 + [tok(LANES), tok(512), tok(512),
                   pl.BlockSpec((1, 1, 1, 512), lambda b, i: (b, i, 0, 0)),
                   tokt(MOBA_HEADS * VT_HEAD), tokt(VT_HEAD), tokt(VT_HEAD)],
        out_shape=[sds(512, BF16)] + [sds(HEAD_DIM, BF16)] * 4 + [sds(LANES, F32), sds(512, BF16), sds(512, BF16),
                   jax.ShapeDtypeStruct((B, nb, 1, 512), F32), sdst(MOBA_HEADS * VT_HEAD), sdst(VT_HEAD),
                   sdst(VT_HEAD)],
        compiler_params=_cparams(("parallel", "parallel")),
        name="in_proj",
    )(x, mod3, g_attn.reshape(1, D), w, wvt, cq, sq, ckv, skv)


def _compress_kernel(kr_ref, vr_ref, pk_ref, w1k_ref, w2k_ref, pv_ref, w1v_ref, w2vt_ref, ko_ref, vo_ref):
    def hidden(r_ref, p_ref, w1_ref):
        r = r_ref[0]
        half = r.shape[1]
        w1 = w1_ref[...]
        top = _dot(r, w1[:half])
        bot = _dot(r, w1[half:])
        bias = _dot_f32(jnp.broadcast_to(p_ref[...], (SUBLANES, 2 * half)), w1)[0:1]
        n = top.shape[0]
        return _silu(top + pltpu.roll(bot, n - 1, 0) + bias)

    ko_ref[0] = _dot(hidden(kr_ref, pk_ref, w1k_ref), w2k_ref[...]).astype(BF16)
    vo_ref[0] = _dot_nt(w2vt_ref[...], hidden(vr_ref, pv_ref, w1v_ref)).astype(BF16)


def _compress(kc, vc, pos_k, w1_k, w2_k, pos_v, w1_v, w2_v):
    B, S, dh = kc.shape
    ng = S // NSA_CMP_STRIDE
    width = NSA_CMP_STRIDE * dh
    kr = kc.reshape(B, ng, width)
    vr = vc.reshape(B, ng, width)
    full = lambda a: pl.BlockSpec(a.shape, lambda b: (0,) * a.ndim)
    pk = pos_k.reshape(1, NSA_CMP_LEN * dh)
    pv = pos_v.reshape(1, NSA_CMP_LEN * dh)
    args = (kr, vr, pk, w1_k, w2_k, pv, w1_v, w2_v.T)
    return pl.pallas_call(
        _compress_kernel,
        grid=(B,),
        in_specs=[pl.BlockSpec((1, ng, width), lambda b: (b, 0, 0))] * 2 + [full(a) for a in args[2:]],
        out_specs=[pl.BlockSpec((1, ng, dh), lambda b: (b, 0, 0)), pl.BlockSpec((1, dh, ng), lambda b: (b, 0, 0))],
        out_shape=[jax.ShapeDtypeStruct((B, ng, dh), BF16), jax.ShapeDtypeStruct((B, dh, ng), BF16)],
        compiler_params=_cparams(("parallel",)),
        name="nsa_compress",
    )(*args)


def _nsa_cmp_kernel(q_ref, kc_ref, vct_ref, o_ref, selt_ref, *, tq, n_sel):
    qi = pl.program_id(1)
    kc = kc_ref[0]
    vct = vct_ref[0]
    ng = kc.shape[0]
    nidx = lax.broadcasted_iota(I32, (ng, tq), 0)
    t = qi * tq + lax.broadcasted_iota(I32, (ng, tq), 1)
    cmask = (nidx * NSA_CMP_STRIDE + NSA_CMP_LEN - 1 <= t) & (nidx < ng - 1)
    psum = jnp.zeros((ng, tq), F32)
    outs = []
    scores = [_dot_nt(kc, q_ref[0, :, HEAD_DIM * h:HEAD_DIM * (h + 1)]) for h in range(NSA_HEADS)]
    for h in range(NSA_HEADS):
        s = jnp.where(cmask, scores[h], NEG)
        m = jnp.max(s, axis=0, keepdims=True)
        e = jnp.where(cmask, jnp.exp2(s - m), 0.0)
        d = jnp.sum(e, axis=0, keepdims=True)
        p = e / jnp.where(d > 0.0, d, 1.0)
        outs.append(_dot(vct, p).T)
        psum = psum + p
    o_ref[0] = jnp.concatenate(outs, axis=1)
    ss = lax.broadcasted_iota(I32, (n_sel, ng), 0) * NSA_SEL_LEN
    cs = lax.broadcasted_iota(I32, (n_sel, ng), 1) * NSA_CMP_STRIDE
    overlap_t = ((cs < ss + NSA_SEL_LEN) & (cs + NSA_CMP_LEN > ss)).astype(F32)
    imp = _dot_f32(overlap_t, psum)
    j = lax.broadcasted_iota(I32, (n_sel, tq), 0)
    cur = (qi * tq + lax.broadcasted_iota(I32, (n_sel, tq), 1)) // NSA_SEL_LEN
    valid = j <= cur
    forced = valid & ((j == 0) | (j == cur) | (j == cur - 1))
    score = jnp.where(forced, jnp.inf, jnp.where(valid, imp, -jnp.inf))
    sel = _topk_rows(score, min(NSA_SEL_TOPK, n_sel)) & valid
    selt_ref[0] = sel.astype(F32)


def _nsa_cmp(qa, kcmp, vcmp_t):
    B, S, W = qa.shape
    tq = SEQ_TILE
    ng = kcmp.shape[1]
    n_sel = S // NSA_SEL_LEN
    return pl.pallas_call(
        functools.partial(_nsa_cmp_kernel, tq=tq, n_sel=n_sel),
        grid=(B, S // tq),
        in_specs=[pl.BlockSpec((1, tq, W), lambda b, i: (b, i, 0)),
                  pl.BlockSpec((1, ng, HEAD_DIM), lambda b, i: (b, 0, 0)),
                  pl.BlockSpec((1, HEAD_DIM, ng), lambda b, i: (b, 0, 0))],
        out_specs=[pl.BlockSpec((1, tq, W), lambda b, i: (b, i, 0)),
                   pl.BlockSpec((1, n_sel, tq), lambda b, i: (b, 0, i))],
        out_shape=[jax.ShapeDtypeStruct((B, S, W), F32), jax.ShapeDtypeStruct((B, n_sel, S), F32)],
        compiler_params=_cparams(("parallel", "parallel")),
        name="nsa_cmp_select",
    )(qa, kcmp, vcmp_t)


def _nsa_flash_kernel(*refs, mode, tq):
    if mode == "slc":
        q_ref, k_ref, vt_ref, selt_ref, o_ref, qs_ref, m_ref, acc_ref = refs
    else:
        q_ref, k_ref, vt_ref, o_ref, qs_ref, m_ref, acc_ref = refs
    H = NSA_HEADS
    tk = tq
    qi = pl.program_id(1)
    for h in range(H):
        qs_ref[h * tq:(h + 1) * tq, :] = q_ref[0, :, HEAD_DIM * h:HEAD_DIM * (h + 1)]
    m_ref[...] = jnp.full(m_ref.shape, NEG, F32)
    acc_ref[...] = jnp.zeros(acc_ref.shape, F32)
    krow = lax.broadcasted_iota(I32, (tk, tq), 0)
    t_pos = qi * tq + lax.broadcasted_iota(I32, (tk, tq), 1)

    group = H // 4

    def tile_scores(kt):
        k0 = pl.multiple_of(kt * tk, tk)
        k = k_ref[0, pl.ds(k0, tk), :]
        kpos = k0 + krow
        if mode == "slc":
            n_sel = selt_ref.shape[1]
            kblk = (k0 + lax.broadcasted_iota(I32, (tk, n_sel), 0)) // NSA_SEL_LEN
            expand = (kblk == lax.broadcasted_iota(I32, (tk, n_sel), 1)).astype(BF16)
            mask = (_dot(expand, selt_ref[0]) > 0.5) & (kpos <= t_pos)
        else:
            mask = (kpos <= t_pos) & (kpos > t_pos - NSA_WINDOW)
        s_grp = [_dot_nt(k, qs_ref[g * group * tq:(g + 1) * group * tq, :]) for g in range(H // group)]
        return mask, s_grp

    def tile_update(kt, mask, s_grp):
        k0 = pl.multiple_of(kt * tk, tk)
        vt = vt_ref[0, :, pl.ds(k0, tk)]
        for h in range(H):
            hs = slice(h * tq, (h + 1) * tq)
            gs = slice((h % group) * tq, (h % group + 1) * tq)
            s = jnp.where(mask, s_grp[h // group][:, gs], NEG)
            m_prev = m_ref[:, hs]
            m_new = jnp.maximum(m_prev, jnp.max(s, axis=0, keepdims=True))
            alpha = jnp.exp2(m_prev - m_new)
            p = jnp.exp2(s - m_new)
            acc_ref[:, hs] = alpha * acc_ref[:, hs] + _dot(vt, p)
            m_ref[:, hs] = m_new

    tile_update(qi, *tile_scores(qi))
    lo = 0 if mode == "slc" else jnp.maximum(qi - NSA_WINDOW // tk, 0)
    n_past = qi - lo

    def body(j, carry):
        kt = lo + 2 * j
        sc_a = tile_scores(kt)
        sc_b = tile_scores(kt + 1)
        tile_update(kt, *sc_a)
        tile_update(kt + 1, *sc_b)
        return carry

    lax.fori_loop(0, n_past // 2, body, 0)

    @pl.when(n_past % 2 == 1)
    def _():
        tile_update(qi - 1, *tile_scores(qi - 1))
    o_t = acc_ref[0:HEAD_DIM, :] / acc_ref[HEAD_DIM:HEAD_DIM + 1, :]
    o_ref[0] = jnp.concatenate([o_t[:, h * tq:(h + 1) * tq].T for h in range(H)], axis=1)


def _nsa_flash(mode, qa, k, v_t, sel_t=None):
    B, S, W = qa.shape
    tq = SEQ_TILE
    H = NSA_HEADS
    in_specs = [pl.BlockSpec((1, tq, W), lambda b, i: (b, i, 0)),
                pl.BlockSpec((1, S, HEAD_DIM), lambda b, i: (b, 0, 0)),
                pl.BlockSpec((1, VT_HEAD, S), lambda b, i: (b, 0, 0))]
    args = [qa, k, v_t]
    if mode == "slc":
        in_specs.append(pl.BlockSpec((1, sel_t.shape[1], tq), lambda b, i: (b, 0, i)))
        args.append(sel_t)
    return pl.pallas_call(
        functools.partial(_nsa_flash_kernel, mode=mode, tq=tq),
        grid=(B, S // tq),
        in_specs=in_specs,
        out_specs=pl.BlockSpec((1, tq, W), lambda b, i: (b, i, 0)),
        out_shape=jax.ShapeDtypeStruct((B, S, W), F32),
        scratch_shapes=[pltpu.VMEM((H * tq, HEAD_DIM), BF16),
                        pltpu.VMEM((1, H * tq), F32),
                        pltpu.VMEM((VT_HEAD, H * tq), F32)],
        compiler_params=_cparams(("parallel", "parallel")),
        name="nsa_flash_" + mode,
    )(*args)


def _moba_kernel(q_ref, k_ref, vt_ref, km_ref, o_ref, qs_ref, sel_ref, m_ref, acc_ref, *, nb):
    tq = MOBA_BLOCK
    H = MOBA_HEADS
    own = pl.program_id(1)
    own0 = pl.multiple_of(own * tq, tq)
    krow = lax.broadcasted_iota(I32, (tq, tq), 0)
    qcol = lax.broadcasted_iota(I32, (tq, tq), 1)
    blk = lax.broadcasted_iota(I32, (nb, tq), 0)
    heads = [slice(HEAD_DIM * h, HEAD_DIM * (h + 1)) for h in range(H)]
    vheads = [slice(VT_HEAD * h, VT_HEAD * (h + 1)) for h in range(H)]
    for h, sl in enumerate(heads):
        qs_ref[h] = q_ref[0, :, sl]
    gates = [_dot_nt_f32(km_ref[0, :, sl], qs_ref[h]) for h, sl in enumerate(heads)]
    scores = [_dot_nt(k_ref[0, pl.ds(own0, tq), sl], qs_ref[h]) for h, sl in enumerate(heads)]
    for h in range(H):
        gate = jnp.where(blk < own, gates[h], -jnp.inf)
        sel_ref[h] = (_topk_rows(gate, min(MOBA_TOPK, nb)) & (blk < own)).astype(F32)
        s = jnp.where(krow <= qcol, scores[h], NEG)
        m = jnp.max(s, axis=0, keepdims=True)
        m_ref[h] = m
        acc_ref[h] = _dot(vt_ref[0, vheads[h], pl.ds(own0, tq)], jnp.exp2(s - m))

    def block_scores(n):
        n0 = pl.multiple_of(n * tq, tq)
        return [_dot_nt(k_ref[0, pl.ds(n0, tq), sl], qs_ref[h]) for h, sl in enumerate(heads)]

    def block_update(n, scores):
        n0 = pl.multiple_of(n * tq, tq)
        for h in range(H):
            chosen = sel_ref[h, pl.ds(n, 1), :] > 0.5
            s = scores[h]
            m_prev = m_ref[h]
            m_new = jnp.where(chosen, jnp.maximum(m_prev, jnp.max(s, axis=0, keepdims=True)), m_prev)
            alpha = jnp.exp2(m_prev - m_new)
            p = jnp.exp2(s - m_new)
            pv = _dot(vt_ref[0, vheads[h], pl.ds(n0, tq)], p)
            acc_ref[h] = alpha * acc_ref[h] + jnp.where(chosen, pv, 0.0)
            m_ref[h] = m_new

    def body(j, carry):
        s_a = block_scores(2 * j)
        s_b = block_scores(2 * j + 1)
        block_update(2 * j, s_a)
        block_update(2 * j + 1, s_b)
        return carry

    lax.fori_loop(0, own // 2, body, 0)

    @pl.when(own % 2 == 1)
    def _():
        block_update(own - 1, block_scores(own - 1))
    o_ref[0] = jnp.concatenate([(acc_ref[h, 0:HEAD_DIM] / acc_ref[h, HEAD_DIM:HEAD_DIM + 1]).T for h in range(H)],
                               axis=1)


def _moba(qb, kb, vb_t, kmean):
    B, S, W = qb.shape
    tq = MOBA_BLOCK
    nb = S // tq
    H = MOBA_HEADS
    return pl.pallas_call(
        functools.partial(_moba_kernel, nb=nb),
        grid=(B, nb),
        in_specs=[pl.BlockSpec((1, tq, W), lambda b, i: (b, i, 0)),
                  pl.BlockSpec((1, S, W), lambda b, i: (b, 0, 0)),
                  pl.BlockSpec((1, H * VT_HEAD, S), lambda b, i: (b, 0, 0)),
                  pl.BlockSpec((1, nb, W), lambda b, i: (b, 0, 0))],
        out_specs=pl.BlockSpec((1, tq, W), lambda b, i: (b, i, 0)),
        out_shape=jax.ShapeDtypeStruct((B, S, W), F32),
        scratch_shapes=[pltpu.VMEM((H, tq, HEAD_DIM), BF16),
                        pltpu.VMEM((H, nb, tq), F32),
                        pltpu.VMEM((H, 1, tq), F32),
                        pltpu.VMEM((H, VT_HEAD, tq), F32)],
        compiler_params=_cparams(("parallel", "parallel")),
        name="moba",
    )(qb, kb, vb_t, kmean)


def _post_attn_kernel(x_ref, mod_ref, oc_ref, os_ref, ow_ref, gt_ref, ob_ref, wo_ref, gf_ref, wrt_ref, rb_ref,
                      wsg_ref, wsu_ref, wsd_ref, h2_ref, pre_ref, eidx_ref, wts_ref, hot_ref, cnt_ref):
    first = (pl.program_id(0) == 0) & (pl.program_id(1) == 0)
    x = x_ref[0]
    tm = x.shape[0]
    gt1 = mod_ref[0, 2:3, :]
    sh2 = mod_ref[0, 3:4, :]
    sc2 = mod_ref[0, 4:5, :]
    gt2 = mod_ref[0, 5:6, :]
    gates = gt_ref[0]
    gi = lax.broadcasted_iota(I32, (LANES, 512), 0)
    hi = lax.broadcasted_iota(I32, (LANES, 512), 1) // HEAD_DIM
    o_a = jnp.zeros((tm, 512), F32)
    for jbr, ref in enumerate((oc_ref, os_ref, ow_ref)):
        expand = (gi == 3 * hi + jbr).astype(F32)
        o_a = o_a + _dot_f32(gates, expand) * ref[0]
    wo = wo_ref[...]
    attn = _dot(o_a, wo[:512]) + _dot(ob_ref[0], wo[512:])
    h1 = x + gt1 * attn
    h2 = (_rms(h1, gf_ref[...]) * (1.0 + sc2) + sh2).astype(BF16)
    half = h2.shape[1] // 2
    lo = lax.bitcast_convert_type(h2[:, :half].astype(F32), U32) >> 16
    hi = lax.bitcast_convert_type(h2[:, half:].astype(F32), U32) & jnp.uint32(0xFFFF0000)
    words = lo | hi
    for s in range(PACKED_SUBLANES):
        h2_ref[pl.ds(s, tm, stride=PACKED_SUBLANES), :] = words[:, LANES * s:LANES * (s + 1)]
    shared = _dot(_silu(_dot(h2, wsg_ref[...])) * _dot(h2, wsu_ref[...]), wsd_ref[...])
    pre = h1 + gt2 * shared
    for s in range(SUBLANES):
        pre_ref[pl.ds(s, tm, stride=SUBLANES), :] = pre[:, LANES * s:LANES * (s + 1)]
    scores = jax.nn.sigmoid(_dot_nt(wrt_ref[...], h2))
    biased = scores + rb_ref[...]
    E = N_EXPERTS
    per = E // N_GROUPS
    rowp = lax.broadcasted_iota(I32, (per, tm), 0)
    gs, vals = [], []
    for g in range(N_GROUPS):
        v = biased[per * g:per * (g + 1), :]
        m1 = jnp.max(v, axis=0, keepdims=True)
        i1 = jnp.min(jnp.where(v == m1, rowp, per), axis=0, keepdims=True)
        m2 = jnp.max(jnp.where(rowp == i1, -jnp.inf, v), axis=0, keepdims=True)
        gs.append(m1 + m2)
        vals.append(v)
    cands = []
    for g in range(N_GROUPS):
        rank = jnp.zeros((1, tm), I32)
        for i in range(N_GROUPS):
            beats = (gs[i] > gs[g]) | ((gs[i] == gs[g]) & (i < g))
            rank = rank + beats.astype(I32)
        cands.append(jnp.where(rank < TOPK_GROUPS, vals[g], -jnp.inf))
    cand = jnp.concatenate(cands, axis=0)
    row = lax.broadcasted_iota(I32, (E, tm), 0)
    krow = lax.broadcasted_iota(I32, (TOP_K, tm), 0)
    eidx = jnp.zeros((TOP_K, tm), I32)
    wsel = jnp.zeros((TOP_K, tm), F32)
    hot = jnp.zeros((E, tm), F32)
    wsum = jnp.zeros((1, tm), F32)
    for kk in range(TOP_K):
        m = jnp.max(cand, axis=0, keepdims=True)
        idx = jnp.min(jnp.where(cand == m, row, E), axis=0, keepdims=True)
        pick = row == idx
        w = jnp.sum(jnp.where(pick, scores, 0.0), axis=0, keepdims=True)
        cand = jnp.where(pick, -jnp.inf, cand)
        hot = hot + pick.astype(F32)
        eidx = jnp.where(krow == kk, idx, eidx)
        wsel = jnp.where(krow == kk, w, wsel)
        wsum = wsum + w
    eidx_ref[...] = eidx
    wts_ref[...] = wsel / wsum * ROUTED_SCALE
    hot_ref[...] = hot.astype(BF16)

    @pl.when(first)
    def _():
        cnt_ref[...] = jnp.zeros(cnt_ref.shape, F32)

    cnt_ref[...] += jnp.sum(hot, axis=1, keepdims=True)


def _post_attn(x, mod3, o_cmp, o_slc, o_win, gates, o_b, w_out, g_ffn, w_router, router_bias, w_sg, w_su, w_sd):
    B, S, D = x.shape
    tm = SEQ_TILE
    nt = S // tm
    T = B * S
    E = N_EXPERTS
    tok = lambda w_: pl.BlockSpec((1, tm, w_), lambda b, i: (b, i, 0))
    full = lambda a: pl.BlockSpec(a.shape, lambda b, i: (0,) * a.ndim)
    rows = lambda r_: pl.BlockSpec((tm * r_, LANES), lambda b, i: (b * nt + i, 0))
    cols = lambda r_: pl.BlockSpec((r_, tm), lambda b, i: (0, b * nt + i))
    consts = (w_out.astype(BF16), g_ffn.reshape(1, D), w_router.T.astype(BF16), router_bias.reshape(E, 1),
              w_sg.astype(BF16), w_su.astype(BF16), w_sd.astype(BF16))
    return pl.pallas_call(
        _post_attn_kernel,
        grid=(B, nt),
        in_specs=[tok(D), pl.BlockSpec((1, 6, D), lambda b, i: (b, 0, 0)),
                  tok(512), tok(512), tok(512), tok(LANES), tok(512)] + [full(a) for a in consts],
        out_specs=[rows(PACKED_SUBLANES), rows(SUBLANES), cols(TOP_K), cols(TOP_K), cols(E),
                   pl.BlockSpec((E, 1), lambda b, i: (0, 0))],
        out_shape=[jax.ShapeDtypeStruct((T * PACKED_SUBLANES, LANES), U32),
                   jax.ShapeDtypeStruct((T * SUBLANES, LANES), F32),
                   jax.ShapeDtypeStruct((TOP_K, T), I32), jax.ShapeDtypeStruct((TOP_K, T), F32),
                   jax.ShapeDtypeStruct((E, T), BF16), jax.ShapeDtypeStruct((E, 1), F32)],
        compiler_params=_cparams(("arbitrary", "arbitrary")),
        name="post_attn_router",
    )(x, mod3, o_cmp, o_slc, o_win, gates, o_b, *consts)


def _moe_pos_kernel(hot_ref, eidx_ref, start_ref, dest_ref, carry_ref):
    @pl.when(pl.program_id(0) == 0)
    def _():
        carry_ref[...] = start_ref[...]

    hot = hot_ref[...]
    E, tp = hot.shape
    r = lax.broadcasted_iota(I32, (tp, tp), 0)
    c = lax.broadcasted_iota(I32, (tp, tp), 1)
    earlier = (r < c).astype(BF16)
    pos = carry_ref[...] + _dot(hot, earlier)
    row = lax.broadcasted_iota(I32, (E, tp), 0)
    krow = lax.broadcasted_iota(I32, (TOP_K, tp), 0)
    eidx = eidx_ref[...]
    dest = jnp.zeros((TOP_K, tp), I32)
    for kk in range(TOP_K):
        d = jnp.sum(jnp.where(row == eidx[kk:kk + 1, :], pos, 0.0), axis=0, keepdims=True)
        dest = jnp.where(krow == kk, d.astype(I32), dest)
    dest_ref[...] = dest
    carry_ref[...] += jnp.sum(hot.astype(F32), axis=1, keepdims=True)


def _moe_pos(hot, eidx, starts):
    E, T = hot.shape
    tp = 512 if T % 512 == 0 else SEQ_TILE
    return pl.pallas_call(
        _moe_pos_kernel,
        grid=(T // tp,),
        in_specs=[pl.BlockSpec((E, tp), lambda i: (0, i)),
                  pl.BlockSpec((TOP_K, tp), lambda i: (0, i)),
                  pl.BlockSpec((E, 1), lambda i: (0, 0))],
        out_specs=pl.BlockSpec((TOP_K, tp), lambda i: (0, i)),
        out_shape=jax.ShapeDtypeStruct((TOP_K, T), I32),
        scratch_shapes=[pltpu.VMEM((E, 1), F32)],
        compiler_params=_cparams(("arbitrary",)),
        name="moe_pos",
    )(hot, eidx, starts)


def _row(ref, start, n=SUBLANES):
    return ref.at[pl.ds(pl.multiple_of(start, n), n)]


def _moe_scatter_kernel(dest_ref, last_ref, h_ref, xs_ref, zbuf, sem, zsem):
    n = PACKED_SUBLANES
    ts = h_ref.shape[0] // n

    @pl.when(pl.program_id(0) == 0)
    def _():
        zbuf[...] = jnp.zeros(zbuf.shape, U32)

        def zero_copy(e):
            return pltpu.make_async_copy(zbuf, xs_ref.at[pl.ds(pl.multiple_of(last_ref[e], EXPERT_ROWS),
                                                               EXPERT_ROWS)], zsem)

        def zstart(e, carry):
            @pl.when(last_ref[e] >= 0)
            def _():
                zero_copy(e).start()
            return carry

        def zwait(e, carry):
            @pl.when(last_ref[e] >= 0)
            def _():
                zero_copy(e).wait()
            return carry

        lax.fori_loop(0, N_EXPERTS, zstart, 0)
        lax.fori_loop(0, N_EXPERTS, zwait, 0)

    def copy(r, kk):
        return pltpu.make_async_copy(_row(h_ref, r * n, n), xs_ref.at[dest_ref[r * TOP_K + kk]], sem)

    def issue(r, carry):
        for kk in range(TOP_K):
            copy(r, kk).start(priority=kk % 2)
        return carry

    def drain(r, carry):
        for kk in range(TOP_K):
            copy(r, kk).wait()
        return carry

    lax.fori_loop(0, ts, issue, 0)
    lax.fori_loop(0, ts, drain, 0)


def _moe_scatter(h2p, dest, last_block_row, n_rows):
    n = PACKED_SUBLANES
    T = h2p.shape[0] // n
    ts = MOE_TILE
    return pl.pallas_call(
        _moe_scatter_kernel,
        grid=(T // ts,),
        in_specs=[pl.BlockSpec((ts * TOP_K,), lambda i: (i,), memory_space=pltpu.SMEM),
                  pl.BlockSpec((N_EXPERTS,), lambda i: (0,), memory_space=pltpu.SMEM),
                  pl.BlockSpec((ts * n, LANES), lambda i: (i, 0))],
        out_specs=pl.BlockSpec(memory_space=pl.ANY),
        out_shape=jax.ShapeDtypeStruct((n_rows, n, LANES), U32),
        scratch_shapes=[pltpu.VMEM((EXPERT_ROWS, n, LANES), U32), pltpu.SemaphoreType.DMA(()),
                        pltpu.SemaphoreType.DMA(())],
        compiler_params=_cparams(("arbitrary",)),
        name="moe_scatter",
    )(dest, last_block_row, h2p)


def _moe_experts_kernel(be_ref, bf_ref, nx_ref, wp_ref, nu_ref, xs_ref, wg_ref, wu_ref, wd_ref, ys_ref,
                        wg_s, wu_s, wd_s, wg_f, wu_f, wd_f, xbuf, ybuf, sem_in, sem_out, sem_w):
    i = pl.program_id(0)
    n_used = nu_ref[0]
    rows = EXPERT_ROWS
    slot = i % 2
    islot = i % IN_SLOTS

    def w_copies(e, sl):
        return [pltpu.make_async_copy(src.at[e], dst.at[sl], sem_w.at[sl])
                for src, dst in ((wg_ref, wg_f), (wu_ref, wu_f), (wd_ref, wd_f))]

    def in_copies(blk):
        r0 = pl.multiple_of(blk * rows, rows)
        sl = blk % IN_SLOTS
        return [pltpu.make_async_copy(xs_ref.at[pl.ds(r0, rows), s], xbuf.at[sl, s], sem_in.at[sl])
                for s in range(PACKED_SUBLANES)]

    def out_copies(blk, sl):
        r0 = pl.multiple_of(blk * rows, rows)
        return [pltpu.make_async_copy(ybuf.at[sl, s], ys_ref.at[pl.ds(r0, rows), s], sem_out.at[sl])
                for s in range(SUBLANES)]

    @pl.when(i == 0)
    def _():
        for cp in w_copies(be_ref[0], 0):
            cp.start()
        for ahead in range(IN_SLOTS - 1):
            @pl.when(ahead < n_used)
            def _():
                for cp in in_copies(ahead):
                    cp.start()

    @pl.when(i + IN_SLOTS - 1 < n_used)
    def _():
        for cp in in_copies(i + IN_SLOTS - 1):
            cp.start()

    @pl.when((bf_ref[i] == 1) & (i < n_used))
    def _():
        wslot = wp_ref[i]
        for cp in w_copies(be_ref[i], wslot):
            cp.wait()
        wg_s[...] = wg_f[wslot].astype(BF16)
        wu_s[...] = wu_f[wslot].astype(BF16)
        wd_s[...] = wd_f[wslot].astype(BF16)

        @pl.when(nx_ref[i] >= 0)
        def _():
            for cp in w_copies(nx_ref[i], 1 - wslot):
                cp.start()

    @pl.when(i < n_used)
    def _():
        for cp in in_copies(i):
            cp.wait()

        @pl.when(i >= 2)
        def _():
            for cp in out_copies(i - 2, slot):
                cp.wait()

        words = [xbuf[islot, s] for s in range(PACKED_SUBLANES)]
        lo = [lax.bitcast_convert_type(w << 16, F32).astype(BF16) for w in words]
        hi = [lax.bitcast_convert_type(w & jnp.uint32(0xFFFF0000), F32).astype(BF16) for w in words]
        x = jnp.concatenate(lo + hi, axis=1)
        a = _silu(_dot(x, wg_s[...])) * _dot(x, wu_s[...])
        y = _dot(a, wd_s[...])
        for s in range(SUBLANES):
            ybuf[slot, s] = y[:, LANES * s:LANES * (s + 1)]
        for cp in out_copies(i, slot):
            cp.start()

        @pl.when(i == n_used - 1)
        def _():
            for cp in out_copies(i, slot):
                cp.wait()

            @pl.when(i >= 1)
            def _():
                for cp in out_copies(i - 1, 1 - slot):
                    cp.wait()


def _moe_experts(xs, blk_exp, blk_first, blk_next, blk_wslot, n_used, w_eg, w_eu, w_ed):
    n_rows = xs.shape[0]
    E, D, De = w_eg.shape
    n_blk = n_rows // EXPERT_ROWS
    grid_spec = pltpu.PrefetchScalarGridSpec(
        num_scalar_prefetch=5,
        grid=(n_blk,),
        in_specs=[pl.BlockSpec(memory_space=pl.ANY)] * 4,
        out_specs=pl.BlockSpec(memory_space=pl.ANY),
        scratch_shapes=[pltpu.VMEM((D, De), BF16), pltpu.VMEM((D, De), BF16), pltpu.VMEM((De, D), BF16),
                        pltpu.VMEM((2, D, De), F32), pltpu.VMEM((2, D, De), F32), pltpu.VMEM((2, De, D), F32),
                        pltpu.VMEM((IN_SLOTS, PACKED_SUBLANES, EXPERT_ROWS, LANES), U32),
                        pltpu.VMEM((2, SUBLANES, EXPERT_ROWS, LANES), F32),
                        pltpu.SemaphoreType.DMA((IN_SLOTS,)), pltpu.SemaphoreType.DMA((2,)),
                        pltpu.SemaphoreType.DMA((2,))],
    )
    return pl.pallas_call(
        _moe_experts_kernel,
        grid_spec=grid_spec,
        out_shape=jax.ShapeDtypeStruct((n_rows, SUBLANES, LANES), F32),
        compiler_params=_cparams(("arbitrary",)),
        name="moe_experts",
    )(blk_exp, blk_first, blk_next, blk_wslot, n_used, xs, w_eg, w_eu, w_ed)


def _moe_combine_kernel(dcur_ref, dnxt_ref, w_ref, pre_ref, gt2_ref, gfin_ref, ys_ref, o_ref, buf, hbuf, sem):
    i = pl.program_id(0)
    n = pl.num_programs(0)
    tc = pre_ref.shape[0] // SUBLANES
    slot = i % 2

    def copy(d_ref, sl, j):
        return pltpu.make_async_copy(ys_ref.at[d_ref[j]], _row(buf.at[sl], j * SUBLANES), sem.at[sl])

    def issue(d_ref, sl):
        def body(r2, carry):
            for u in range(2 * TOP_K):
                copy(d_ref, sl, r2 * (2 * TOP_K) + u).start(priority=u % 2)
            return carry
        lax.fori_loop(0, tc // 2, body, 0)

    @pl.when(i == 0)
    def _():
        issue(dcur_ref, 0)

    @pl.when(i + 1 < n)
    def _():
        issue(dnxt_ref, 1 - slot)

    def drain(r, carry):
        for kk in range(TOP_K):
            copy(dcur_ref, slot, r * TOP_K + kk).wait()
        return carry

    lax.fori_loop(0, tc, drain, 0)
    cur = buf.at[slot]
    gt2 = gt2_ref[...]
    unroll = 4

    def tokens(g, carry):
        for u in range(unroll):
            r = g * unroll + u
            routed = jnp.zeros((SUBLANES, LANES), F32)
            for kk in range(TOP_K):
                j = r * TOP_K + kk
                routed = routed + w_ref[j] * _row(cur, j * SUBLANES)[...]
            _row(hbuf, r * SUBLANES)[...] = _row(pre_ref, r * SUBLANES)[...] + gt2 * routed
        return carry

    lax.fori_loop(0, tc // unroll, tokens, 0)
    h = jnp.concatenate([hbuf[pl.ds(s, tc, stride=SUBLANES), :] for s in range(SUBLANES)], axis=1)
    o_ref[...] = _rms(h, gfin_ref[...])


def _moe_combine(ys, dest8, w_flat, pre2d, gt2_2d, gfin, S):
    T = pre2d.shape[0] // SUBLANES
    tc = MOE_TILE
    per_b = S // tc
    n = T // tc
    return pl.pallas_call(
        _moe_combine_kernel,
        grid=(n,),
        in_specs=[pl.BlockSpec((tc * TOP_K,), lambda i: (i,), memory_space=pltpu.SMEM),
                  pl.BlockSpec((tc * TOP_K,), lambda i: (jnp.minimum(i + 1, n - 1),), memory_space=pltpu.SMEM),
                  pl.BlockSpec((tc * TOP_K,), lambda i: (i,), memory_space=pltpu.SMEM),
                  pl.BlockSpec((tc * SUBLANES, LANES), lambda i: (i, 0)),
                  pl.BlockSpec((SUBLANES, LANES), lambda i: (i // per_b, 0)),
                  pl.BlockSpec((1, SUBLANES * LANES), lambda i: (0, 0)),
                  pl.BlockSpec(memory_space=pl.ANY)],
        out_specs=pl.BlockSpec((tc, SUBLANES * LANES), lambda i: (i, 0)),
        out_shape=jax.ShapeDtypeStruct((T, SUBLANES * LANES), F32),
        scratch_shapes=[pltpu.VMEM((2, tc * TOP_K * SUBLANES, LANES), F32),
                        pltpu.VMEM((tc * SUBLANES, LANES), F32),
                        pltpu.SemaphoreType.DMA((2,))],
        compiler_params=_cparams(("arbitrary",)),
        name="moe_combine",
    )(dest8, dest8, w_flat, pre2d, gt2_2d, gfin, ys)


def _layer(h_res, mod, g_attn, w_in, cmp_pos_k, cmp_w1_k, cmp_w2_k, cmp_pos_v, cmp_w1_v, cmp_w2_v,
           w_out, g_ffn, w_router, router_bias, w_eg, w_eu, w_ed, w_sg, w_su, w_sd):
    B, S, D = h_res.shape
    T = B * S
    mod3 = mod.reshape(B, 6, D)
    (qa, kc, vc, ks, kw, gates, qb, kb, kmean, vb_t, vs_t, vw_t) = _in_proj(h_res, mod3, g_attn, w_in)
    kcmp, vcmp_t = _compress(kc, vc, cmp_pos_k, cmp_w1_k, cmp_w2_k, cmp_pos_v, cmp_w1_v, cmp_w2_v)
    o_cmp, sel_t = _nsa_cmp(qa, kcmp, vcmp_t)
    o_slc = _nsa_flash("slc", qa, ks, vs_t, sel_t)
    o_win = _nsa_flash("win", qa, kw, vw_t)
    o_b = _moba(qb, kb, vb_t, kmean.reshape(B, S // MOBA_BLOCK, 512))
    h2p, pre2d, eidx, wts, hot, counts = _post_attn(h_res, mod3, o_cmp, o_slc, o_win, gates, o_b, w_out, g_ffn,
                                                    w_router, router_bias, w_sg, w_su, w_sd)
    counts = counts.reshape(N_EXPERTS).astype(I32)
    padded = (counts + EXPERT_ROWS - 1) // EXPERT_ROWS * EXPERT_ROWS
    ends = jnp.cumsum(padded)
    starts = ends - padded
    n_blk = -(-(T * TOP_K) // EXPERT_ROWS) + N_EXPERTS
    n_rows = n_blk * EXPERT_ROWS
    blk_row = jnp.arange(n_blk, dtype=I32) * EXPERT_ROWS
    blk_exp = jnp.minimum(jnp.sum((ends[None, :] <= blk_row[:, None]).astype(I32), axis=1), N_EXPERTS - 1)
    blk_first = jnp.concatenate([jnp.ones((1,), I32), (blk_exp[1:] != blk_exp[:-1]).astype(I32)])
    n_used = (ends[-1:] // EXPERT_ROWS).astype(I32)
    blk_idx = jnp.arange(n_blk, dtype=I32)
    first_used = (blk_first == 1) & (blk_idx < n_used[0])
    first_pos = jnp.where(first_used, blk_idx, n_blk)
    next_pos = jnp.concatenate([lax.cummin(first_pos, reverse=True)[1:], jnp.full((1,), n_blk, I32)])
    blk_next = jnp.where(next_pos < n_blk, blk_exp[jnp.minimum(next_pos, n_blk - 1)], -1).astype(I32)
    blk_wslot = ((jnp.cumsum(first_used.astype(I32)) - 1) % 2).astype(I32)
    dest = _moe_pos(hot, eidx, starts.astype(F32).reshape(N_EXPERTS, 1))
    dest_flat = dest.T.reshape(T * TOP_K)
    w_flat = wts.T.reshape(T * TOP_K)
    last_block_row = jnp.where(counts > 0, ends - EXPERT_ROWS, -1).astype(I32)
    xs = _moe_scatter(h2p, dest_flat, last_block_row, n_rows)
    ys = _moe_experts(xs, blk_exp, blk_first, blk_next, blk_wslot, n_used, w_eg, w_eu, w_ed)
    gt2_2d = mod3[:, 5, :].reshape(B * SUBLANES, LANES)
    return ys, dest_flat, w_flat, pre2d, gt2_2d


def kernel(x, c, w_ada, b_ada, g_attn, w_in, cmp_pos_k, cmp_w1_k, cmp_w2_k, cmp_pos_v, cmp_w1_v, cmp_w2_v, w_out, g_ffn, w_router, router_bias, w_exp_gate, w_exp_up, w_exp_down, w_sh_gate, w_sh_up, w_sh_down, g_final):
    B, S, D = x.shape
    depth = w_ada.shape[0]
    assert depth == 1, "the final RMSNorm is fused into the single layer's MoE combine"
    assert D == SUBLANES * LANES and S % SEQ_TILE == 0
    l = 0
    mod = _ada_mod(c.astype(F32), w_ada[l], b_ada[l])
    ys, dest8, w_flat, pre2d, gt2_2d = _layer(
        x.astype(F32), mod, g_attn[l], w_in[l], cmp_pos_k[l], cmp_w1_k[l], cmp_w2_k[l], cmp_pos_v[l], cmp_w1_v[l],
        cmp_w2_v[l], w_out[l], g_ffn[l], w_router[l], router_bias[l], w_exp_gate[l], w_exp_up[l], w_exp_down[l],
        w_sh_gate[l], w_sh_up[l], w_sh_down[l])
    out = _moe_combine(ys, dest8, w_flat, pre2d, gt2_2d, g_final.reshape(1, D).astype(F32), S)
    return out.reshape(B, S, D).astype(x.dtype)
```

```python
import functools

import jax
import jax.numpy as jnp
import numpy as np
from jax import lax
from jax.experimental import pallas as pl
from jax.experimental.pallas import tpu as pltpu

F32 = jnp.float32
BF16 = jnp.bfloat16
I32 = jnp.int32
U32 = jnp.uint32

HEAD_DIM = 64
HALF = HEAD_DIM // 2
NSA_HEADS = 8
NSA_CMP_LEN = 32
NSA_CMP_STRIDE = 16
NSA_SEL_LEN = 64
NSA_SEL_TOPK = 16
NSA_WINDOW = 512
MOBA_HEADS = 8
MOBA_BLOCK = 256
MOBA_TOPK = 3
N_EXPERTS = 256
TOP_K = 8
N_GROUPS = 8
TOPK_GROUPS = 4
ROUTED_SCALE = 2.5
ROPE_THETA = 10000.0
EPS = 1e-6
NEG = -1e30
LOG2_E = 1.4426950408889634
ONES_ROWS = 16
VT_HEAD = HEAD_DIM + ONES_ROWS

LANES = 128
SUBLANES = 8
PACKED_SUBLANES = SUBLANES // 2
VMEM_LIMIT = 56 * 1024 * 1024
EXPERT_ROWS = 256
SEQ_TILE = 256
MOE_TILE = 256
IN_SLOTS = 4


def _cparams(sem):
    return pltpu.CompilerParams(dimension_semantics=sem, vmem_limit_bytes=VMEM_LIMIT)


def _dot(a, b):
    return jnp.dot(a.astype(BF16), b.astype(BF16), preferred_element_type=F32)


def _dot_nt(a, b):
    return lax.dot_general(a.astype(BF16), b.astype(BF16), (((1,), (1,)), ((), ())),
                           preferred_element_type=F32)


def _dot_f32(a, b):
    return jnp.dot(a.astype(F32), b.astype(F32), preferred_element_type=F32)


def _dot_nt_f32(a, b):
    return lax.dot_general(a.astype(F32), b.astype(F32), (((1,), (1,)), ((), ())),
                           preferred_element_type=F32)


def _silu(x):
    return x * jax.nn.sigmoid(x)


def _rms(x, g):
    return x * lax.rsqrt(jnp.mean(x * x, axis=-1, keepdims=True) + EPS) * g


def _topk_rows(score, k):
    n = score.shape[0]
    idx = lax.broadcasted_iota(I32, score.shape, 0)
    rank = jnp.zeros(score.shape, I32)
    for i in range(n):
        row = score[i:i + 1, :]
        beats = (row > score) | ((row == score) & (idx > i))
        rank = rank + beats.astype(I32)
    return rank < k


def _ada_kernel(c_ref, w_ref, b_ref, o_ref):
    o_ref[...] = _dot_f32(_silu(c_ref[...]), w_ref[...]) + b_ref[...]


def _ada_mod(c, w, b):
    B, D = c.shape
    N = w.shape[1]
    tn = 1536 if N % 1536 == 0 else N
    return pl.pallas_call(
        _ada_kernel,
        grid=(N // tn,),
        in_specs=[pl.BlockSpec((B, D), lambda j: (0, 0)),
                  pl.BlockSpec((D, tn), lambda j: (0, j)),
                  pl.BlockSpec((1, tn), lambda j: (0, j))],
        out_specs=pl.BlockSpec((B, tn), lambda j: (0, j)),
        out_shape=jax.ShapeDtypeStruct((B, N), F32),
        compiler_params=_cparams(("arbitrary",)),
        name="ada_mod",
    )(c, w, b.reshape(1, N))


_OFF_QA, _OFF_KV, _OFF_G, _OFF_QB, _OFF_KB, _W_COLS = 0, 512, 896, 1024, 1536, 2048
_VT_ROWS = 640


def _swap_halves(t):
    n = t.shape[-1]
    lane = lax.broadcasted_iota(I32, t.shape, t.ndim - 1)
    left = pltpu.roll(t, n - HALF, t.ndim - 1)
    right = pltpu.roll(t, HALF, t.ndim - 1)
    return jnp.where((lane % HEAD_DIM) < HALF, left, right)


def _in_proj_kernel(x_ref, mod_ref, g_ref, w_ref, wvt_ref, cq_ref, sq_ref, ckv_ref, skv_ref,
                    qa_ref, kc_ref, vc_ref, ks_ref, kw_ref, gt_ref, qb_ref, kb_ref, km_ref,
                    vbt_ref, vst_ref, vwt_ref):
    x = x_ref[0]
    sh1 = mod_ref[0, 0:1, :]
    sc1 = mod_ref[0, 1:2, :]
    h = (_rms(x, g_ref[...]) * (1.0 + sc1) + sh1).astype(BF16)
    proj = _dot(h, w_ref[...])
    cq, sq = cq_ref[...], sq_ref[...]
    scale = HEAD_DIM ** -0.5 * LOG2_E

    def rope(t, c, s):
        return t * c + _swap_halves(t) * s

    qa = rope(proj[:, _OFF_QA:_OFF_QA + 512], cq, sq) * scale
    qa_ref[0] = qa.astype(BF16)
    kv = rope(proj[:, _OFF_KV:_OFF_KV + 384], ckv_ref[...], skv_ref[...])
    for j, ref in ((0, kc_ref), (1, vc_ref), (2, ks_ref), (4, kw_ref)):
        ref[0] = kv[:, HEAD_DIM * j:HEAD_DIM * (j + 1)].astype(BF16)
    gt_ref[0] = jax.nn.sigmoid(proj[:, _OFF_G:_OFF_G + LANES])
    qb = rope(proj[:, _OFF_QB:_OFF_QB + 512], cq, sq) * scale
    qb_ref[0] = qb.astype(BF16)
    kb = rope(proj[:, _OFF_KB:_OFF_KB + 512], cq, sq)
    kb_ref[0] = kb.astype(BF16)
    km_ref[0, 0] = jnp.mean(kb, axis=0, keepdims=True)
    vt = _dot_nt(wvt_ref[...], h)
    ones = jnp.ones((ONES_ROWS, vt.shape[1]), BF16)
    for h in range(MOBA_HEADS):
        vbt_ref[0, VT_HEAD * h:VT_HEAD * h + HEAD_DIM] = vt[HEAD_DIM * h:HEAD_DIM * (h + 1)].astype(BF16)
        vbt_ref[0, VT_HEAD * h + HEAD_DIM:VT_HEAD * (h + 1)] = ones
    for ref, r0 in ((vst_ref, 512), (vwt_ref, 576)):
        ref[0, 0:HEAD_DIM] = vt[r0:r0 + HEAD_DIM].astype(BF16)
        ref[0, HEAD_DIM:VT_HEAD] = ones


def _rope_tables(S):
    inv = ROPE_THETA ** (-jnp.arange(0, HEAD_DIM, 2, dtype=F32) / HEAD_DIM)
    ang = jnp.arange(S, dtype=F32)[:, None] * inv[None, :]
    cos, sin = jnp.cos(ang), jnp.sin(ang)
    c_head = jnp.concatenate([cos, cos], axis=-1)
    s_head = jnp.concatenate([-sin, sin], axis=-1)
    cq = jnp.tile(c_head, (1, 8))
    sq = jnp.tile(s_head, (1, 8))
    one, zero = jnp.ones_like(c_head), jnp.zeros_like(s_head)
    ckv = jnp.concatenate([c_head, one, c_head, one, c_head, one], axis=-1)
    skv = jnp.concatenate([s_head, zero, s_head, zero, s_head, zero], axis=-1)
    return cq, sq, ckv, skv


def _in_proj(x, mod3, g_attn, w_in):
    B, S, D = x.shape
    tm = SEQ_TILE
    nb = S // tm
    offs = np.cumsum((512, 64, 64, 64, 64, 64, 64, 24, 512, 512))
    cols = jnp.split(w_in, offs.tolist(), axis=1)
    g_w = jnp.pad(cols[7], ((0, 0), (0, LANES - cols[7].shape[1])))
    w = jnp.concatenate([cols[0]] + cols[1:7] + [g_w, cols[8], cols[9]], axis=1).astype(BF16)
    wvt = jnp.concatenate([cols[10], cols[4], cols[6]], axis=1).T.astype(BF16)
    cq, sq, ckv, skv = _rope_tables(S)
    tok = lambda w_: pl.BlockSpec((1, tm, w_), lambda b, i: (b, i, 0))
    tokt = lambda r_: pl.BlockSpec((1, r_, tm), lambda b, i: (b, 0, i))
    tab = lambda w_: pl.BlockSpec((tm, w_), lambda b, i: (i, 0))
    sds = lambda w_, dt: jax.ShapeDtypeStruct((B, S, w_), dt)
    sdst = lambda r_: jax.ShapeDtypeStruct((B, r_, S), BF16)
    return pl.pallas_call(
        _in_proj_kernel,
        grid=(B, nb),
        in_specs=[tok(D),
                  pl.BlockSpec((1, 6, D), lambda b, i: (b, 0, 0)),
                  pl.BlockSpec((1, D), lambda b, i: (0, 0)),
                  pl.BlockSpec((D, _W_COLS), lambda b, i: (0, 0)),
                  pl.BlockSpec((_VT_ROWS, D), lambda b, i: (0, 0)),
                  tab(512), tab(512), tab(384), tab(384)],
        out_specs=[tok(512)] + [tok(HEAD_DIM)] * 4 + [tok(LANES), tok(512), tok(512),
                   pl.BlockSpec((1, 1, 1, 512), lambda b, i: (b, i, 0, 0)),
                   tokt(MOBA_HEADS * VT_HEAD), tokt(VT_HEAD), tokt(VT_HEAD)],
        out_shape=[sds(512, BF16)] + [sds(HEAD_DIM, BF16)] * 4 + [sds(LANES, F32), sds(512, BF16), sds(512, BF16),
                   jax.ShapeDtypeStruct((B, nb, 1, 512), F32), sdst(MOBA_HEADS * VT_HEAD), sdst(VT_HEAD),
                   sdst(VT_HEAD)],
        compiler_params=_cparams(("parallel", "parallel")),
        name="in_proj",
    )(x, mod3, g_attn.reshape(1, D), w, wvt, cq, sq, ckv, skv)


def _compress_kernel(kr_ref, vr_ref, pk_ref, w1k_ref, w2k_ref, pv_ref, w1v_ref, w2vt_ref, ko_ref, vo_ref):
    def hidden(r_ref, p_ref, w1_ref):
        r = r_ref[0]
        half = r.shape[1]
        w1 = w1_ref[...]
        top = _dot(r, w1[:half])
        bot = _dot(r, w1[half:])
        bias = _dot_f32(jnp.broadcast_to(p_ref[...], (SUBLANES, 2 * half)), w1)[0:1]
        n = top.shape[0]
        return _silu(top + pltpu.roll(bot, n - 1, 0) + bias)

    ko_ref[0] = _dot(hidden(kr_ref, pk_ref, w1k_ref), w2k_ref[...]).astype(BF16)
    vo_ref[0] = _dot_nt(w2vt_ref[...], hidden(vr_ref, pv_ref, w1v_ref)).astype(BF16)


def _compress(kc, vc, pos_k, w1_k, w2_k, pos_v, w1_v, w2_v):
    B, S, dh = kc.shape
    ng = S // NSA_CMP_STRIDE
    width = NSA_CMP_STRIDE * dh
    kr = kc.reshape(B, ng, width)
    vr = vc.reshape(B, ng, width)
    full = lambda a: pl.BlockSpec(a.shape, lambda b: (0,) * a.ndim)
    pk = pos_k.reshape(1, NSA_CMP_LEN * dh)
    pv = pos_v.reshape(1, NSA_CMP_LEN * dh)
    args = (kr, vr, pk, w1_k, w2_k, pv, w1_v, w2_v.T)
    return pl.pallas_call(
        _compress_kernel,
        grid=(B,),
        in_specs=[pl.BlockSpec((1, ng, width), lambda b: (b, 0, 0))] * 2 + [full(a) for a in args[2:]],
        out_specs=[pl.BlockSpec((1, ng, dh), lambda b: (b, 0, 0)), pl.BlockSpec((1, dh, ng), lambda b: (b, 0, 0))],
        out_shape=[jax.ShapeDtypeStruct((B, ng, dh), BF16), jax.ShapeDtypeStruct((B, dh, ng), BF16)],
        compiler_params=_cparams(("parallel",)),
        name="nsa_compress",
    )(*args)


def _nsa_cmp_kernel(q_ref, kc_ref, vct_ref, o_ref, selt_ref, *, tq, n_sel):
    qi = pl.program_id(1)
    kc = kc_ref[0]
    vct = vct_ref[0]
    ng = kc.shape[0]
    nidx = lax.broadcasted_iota(I32, (ng, tq), 0)
    t = qi * tq + lax.broadcasted_iota(I32, (ng, tq), 1)
    cmask = (nidx * NSA_CMP_STRIDE + NSA_CMP_LEN - 1 <= t) & (nidx < ng - 1)
    psum = jnp.zeros((ng, tq), F32)
    outs = []
    scores = [_dot_nt(kc, q_ref[0, :, HEAD_DIM * h:HEAD_DIM * (h + 1)]) for h in range(NSA_HEADS)]
    for h in range(NSA_HEADS):
        s = jnp.where(cmask, scores[h], NEG)
        m = jnp.max(s, axis=0, keepdims=True)
        e = jnp.where(cmask, jnp.exp2(s - m), 0.0)
        d = jnp.sum(e, axis=0, keepdims=True)
        p = e / jnp.where(d > 0.0, d, 1.0)
        outs.append(_dot(vct, p).T)
        psum = psum + p
    o_ref[0] = jnp.concatenate(outs, axis=1)
    ss = lax.broadcasted_iota(I32, (n_sel, ng), 0) * NSA_SEL_LEN
    cs = lax.broadcasted_iota(I32, (n_sel, ng), 1) * NSA_CMP_STRIDE
    overlap_t = ((cs < ss + NSA_SEL_LEN) & (cs + NSA_CMP_LEN > ss)).astype(F32)
    imp = _dot_f32(overlap_t, psum)
    j = lax.broadcasted_iota(I32, (n_sel, tq), 0)
    cur = (qi * tq + lax.broadcasted_iota(I32, (n_sel, tq), 1)) // NSA_SEL_LEN
    valid = j <= cur
    forced = valid & ((j == 0) | (j == cur) | (j == cur - 1))
    score = jnp.where(forced, jnp.inf, jnp.where(valid, imp, -jnp.inf))
    sel = _topk_rows(score, min(NSA_SEL_TOPK, n_sel)) & valid
    selt_ref[0] = sel.astype(F32)


def _nsa_cmp(qa, kcmp, vcmp_t):
    B, S, W = qa.shape
    tq = SEQ_TILE
    ng = kcmp.shape[1]
    n_sel = S // NSA_SEL_LEN
    return pl.pallas_call(
        functools.partial(_nsa_cmp_kernel, tq=tq, n_sel=n_sel),
        grid=(B, S // tq),
        in_specs=[pl.BlockSpec((1, tq, W), lambda b, i: (b, i, 0)),
                  pl.BlockSpec((1, ng, HEAD_DIM), lambda b, i: (b, 0, 0)),
                  pl.BlockSpec((1, HEAD_DIM, ng), lambda b, i: (b, 0, 0))],
        out_specs=[pl.BlockSpec((1, tq, W), lambda b, i: (b, i, 0)),
                   pl.BlockSpec((1, n_sel, tq), lambda b, i: (b, 0, i))],
        out_shape=[jax.ShapeDtypeStruct((B, S, W), F32), jax.ShapeDtypeStruct((B, n_sel, S), F32)],
        compiler_params=_cparams(("parallel", "parallel")),
        name="nsa_cmp_select",
    )(qa, kcmp, vcmp_t)


def _nsa_flash_kernel(*refs, mode, tq):
    if mode == "slc":
        q_ref, k_ref, vt_ref, selt_ref, o_ref, qs_ref, m_ref, acc_ref = refs
    else:
        q_ref, k_ref, vt_ref, o_ref, qs_ref, m_ref, acc_ref = refs
    H = NSA_HEADS
    tk = tq
    qi = pl.program_id(1)
    for h in range(H):
        qs_ref[h * tq:(h + 1) * tq, :] = q_ref[0, :, HEAD_DIM * h:HEAD_DIM * (h + 1)]
    m_ref[...] = jnp.full(m_ref.shape, NEG, F32)
    acc_ref[...] = jnp.zeros(acc_ref.shape, F32)
    krow = lax.broadcasted_iota(I32, (tk, tq), 0)
    t_pos = qi * tq + lax.broadcasted_iota(I32, (tk, tq), 1)

    group = H // 4

    def tile_scores(kt):
        k0 = pl.multiple_of(kt * tk, tk)
        k = k_ref[0, pl.ds(k0, tk), :]
        kpos = k0 + krow
        if mode == "slc":
            n_sel = selt_ref.shape[1]
            kblk = (k0 + lax.broadcasted_iota(I32, (tk, n_sel), 0)) // NSA_SEL_LEN
            expand = (kblk == lax.broadcasted_iota(I32, (tk, n_sel), 1)).astype(BF16)
            mask = (_dot(expand, selt_ref[0]) > 0.5) & (kpos <= t_pos)
        else:
            mask = (kpos <= t_pos) & (kpos > t_pos - NSA_WINDOW)
        s_grp = [_dot_nt(k, qs_ref[g * group * tq:(g + 1) * group * tq, :]) for g in range(H // group)]
        return mask, s_grp

    def tile_update(kt, mask, s_grp):
        k0 = pl.multiple_of(kt * tk, tk)
        vt = vt_ref[0, :, pl.ds(k0, tk)]
        for h in range(H):
            hs = slice(h * tq, (h + 1) * tq)
            gs = slice((h % group) * tq, (h % group + 1) * tq)
            s = jnp.where(mask, s_grp[h // group][:, gs], NEG)
            m_prev = m_ref[:, hs]
            m_new = jnp.maximum(m_prev, jnp.max(s, axis=0, keepdims=True))
            alpha = jnp.exp2(m_prev - m_new)
            p = jnp.exp2(s - m_new)
            acc_ref[:, hs] = alpha * acc_ref[:, hs] + _dot(vt, p)
            m_ref[:, hs] = m_new

    tile_update(qi, *tile_scores(qi))
    lo = 0 if mode == "slc" else jnp.maximum(qi - NSA_WINDOW // tk, 0)
    n_past = qi - lo

    def body(j, carry):
        kt = lo + 2 * j
        sc_a = tile_scores(kt)
        sc_b = tile_scores(kt + 1)
        tile_update(kt, *sc_a)
        tile_update(kt + 1, *sc_b)
        return carry

    lax.fori_loop(0, n_past // 2, body, 0)

    @pl.when(n_past % 2 == 1)
    def _():
        tile_update(qi - 1, *tile_scores(qi - 1))
    o_t = acc_ref[0:HEAD_DIM, :] / acc_ref[HEAD_DIM:HEAD_DIM + 1, :]
    o_ref[0] = jnp.concatenate([o_t[:, h * tq:(h + 1) * tq].T for h in range(H)], axis=1)


def _nsa_flash(mode, qa, k, v_t, sel_t=None):
    B, S, W = qa.shape
    tq = SEQ_TILE
    H = NSA_HEADS
    in_specs = [pl.BlockSpec((1, tq, W), lambda b, i: (b, i, 0)),
                pl.BlockSpec((1, S, HEAD_DIM), lambda b, i: (b, 0, 0)),
                pl.BlockSpec((1, VT_HEAD, S), lambda b, i: (b, 0, 0))]
    args = [qa, k, v_t]
    if mode == "slc":
        in_specs.append(pl.BlockSpec((1, sel_t.shape[1], tq), lambda b, i: (b, 0, i)))
        args.append(sel_t)
    return pl.pallas_call(
        functools.partial(_nsa_flash_kernel, mode=mode, tq=tq),
        grid=(B, S // tq),
        in_specs=in_specs,
        out_specs=pl.BlockSpec((1, tq, W), lambda b, i: (b, i, 0)),
        out_shape=jax.ShapeDtypeStruct((B, S, W), F32),
        scratch_shapes=[pltpu.VMEM((H * tq, HEAD_DIM), BF16),
                        pltpu.VMEM((1, H * tq), F32),
                        pltpu.VMEM((VT_HEAD, H * tq), F32)],
        compiler_params=_cparams(("parallel", "parallel")),
        name="nsa_flash_" + mode,
    )(*args)


def _moba_kernel(q_ref, k_ref, vt_ref, km_ref, o_ref, qs_ref, sel_ref, m_ref, acc_ref, *, nb):
    tq = MOBA_BLOCK
    H = MOBA_HEADS
    own = pl.program_id(1)
    own0 = pl.multiple_of(own * tq, tq)
    krow = lax.broadcasted_iota(I32, (tq, tq), 0)
    qcol = lax.broadcasted_iota(I32, (tq, tq), 1)
    blk = lax.broadcasted_iota(I32, (nb, tq), 0)
    heads = [slice(HEAD_DIM * h, HEAD_DIM * (h + 1)) for h in range(H)]
    vheads = [slice(VT_HEAD * h, VT_HEAD * (h + 1)) for h in range(H)]
    for h, sl in enumerate(heads):
        qs_ref[h] = q_ref[0, :, sl]
    gates = [_dot_nt_f32(km_ref[0, :, sl], qs_ref[h]) for h, sl in enumerate(heads)]
    scores = [_dot_nt(k_ref[0, pl.ds(own0, tq), sl], qs_ref[h]) for h, sl in enumerate(heads)]
    for h in range(H):
        gate = jnp.where(blk < own, gates[h], -jnp.inf)
        sel_ref[h] = (_topk_rows(gate, min(MOBA_TOPK, nb)) & (blk < own)).astype(F32)
        s = jnp.where(krow <= qcol, scores[h], NEG)
        m = jnp.max(s, axis=0, keepdims=True)
        m_ref[h] = m
        acc_ref[h] = _dot(vt_ref[0, vheads[h], pl.ds(own0, tq)], jnp.exp2(s - m))

    def block_scores(n):
        n0 = pl.multiple_of(n * tq, tq)
        return [_dot_nt(k_ref[0, pl.ds(n0, tq), sl], qs_ref[h]) for h, sl in enumerate(heads)]

    def block_update(n, scores):
        n0 = pl.multiple_of(n * tq, tq)
        for h in range(H):
            chosen = sel_ref[h, pl.ds(n, 1), :] > 0.5
            s = scores[h]
            m_prev = m_ref[h]
            m_new = jnp.where(chosen, jnp.maximum(m_prev, jnp.max(s, axis=0, keepdims=True)), m_prev)
            alpha = jnp.exp2(m_prev - m_new)
            p = jnp.exp2(s - m_new)
            pv = _dot(vt_ref[0, vheads[h], pl.ds(n0, tq)], p)
            acc_ref[h] = alpha * acc_ref[h] + jnp.where(chosen, pv, 0.0)
            m_ref[h] = m_new

    def body(j, carry):
        s_a = block_scores(2 * j)
        s_b = block_scores(2 * j + 1)
        block_update(2 * j, s_a)
        block_update(2 * j + 1, s_b)
        return carry

    lax.fori_loop(0, own // 2, body, 0)

    @pl.when(own % 2 == 1)
    def _():
        block_update(own - 1, block_scores(own - 1))
    o_ref[0] = jnp.concatenate([(acc_ref[h, 0:HEAD_DIM] / acc_ref[h, HEAD_DIM:HEAD_DIM + 1]).T for h in range(H)],
                               axis=1)


def _moba(qb, kb, vb_t, kmean):
    B, S, W = qb.shape
    tq = MOBA_BLOCK
    nb = S // tq
    H = MOBA_HEADS
    return pl.pallas_call(
        functools.partial(_moba_kernel, nb=nb),
        grid=(B, nb),
        in_specs=[pl.BlockSpec((1, tq, W), lambda b, i: (b, i, 0)),
                  pl.BlockSpec((1, S, W), lambda b, i: (b, 0, 0)),
                  pl.BlockSpec((1, H * VT_HEAD, S), lambda b, i: (b, 0, 0)),
                  pl.BlockSpec((1, nb, W), lambda b, i: (b, 0, 0))],
        out_specs=pl.BlockSpec((1, tq, W), lambda b, i: (b, i, 0)),
        out_shape=jax.ShapeDtypeStruct((B, S, W), F32),
        scratch_shapes=[pltpu.VMEM((H, tq, HEAD_DIM), BF16),
                        pltpu.VMEM((H, nb, tq), F32),
                        pltpu.VMEM((H, 1, tq), F32),
                        pltpu.VMEM((H, VT_HEAD, tq), F32)],
        compiler_params=_cparams(("parallel", "parallel")),
        name="moba",
    )(qb, kb, vb_t, kmean)


def _post_attn_kernel(x_ref, mod_ref, oc_ref, os_ref, ow_ref, gt_ref, ob_ref, wo_ref, gf_ref, wrt_ref, rb_ref,
                      wsg_ref, wsu_ref, wsd_ref, h2_ref, pre_ref, eidx_ref, wts_ref, hot_ref, cnt_ref):
    first = (pl.program_id(0) == 0) & (pl.program_id(1) == 0)
    x = x_ref[0]
    tm = x.shape[0]
    gt1 = mod_ref[0, 2:3, :]
    sh2 = mod_ref[0, 3:4, :]
    sc2 = mod_ref[0, 4:5, :]
    gt2 = mod_ref[0, 5:6, :]
    gates = gt_ref[0]
    gi = lax.broadcasted_iota(I32, (LANES, 512), 0)
    hi = lax.broadcasted_iota(I32, (LANES, 512), 1) // HEAD_DIM
    o_a = jnp.zeros((tm, 512), F32)
    for jbr, ref in enumerate((oc_ref, os_ref, ow_ref)):
        expand = (gi == 3 * hi + jbr).astype(F32)
        o_a = o_a + _dot_f32(gates, expand) * ref[0]
    wo = wo_ref[...]
    attn = _dot(o_a, wo[:512]) + _dot(ob_ref[0], wo[512:])
    h1 = x + gt1 * attn
    h2 = (_rms(h1, gf_ref[...]) * (1.0 + sc2) + sh2).astype(BF16)
    half = h2.shape[1] // 2
    lo = lax.bitcast_convert_type(h2[:, :half].astype(F32), U32) >> 16
    hi = lax.bitcast_convert_type(h2[:, half:].astype(F32), U32) & jnp.uint32(0xFFFF0000)
    words = lo | hi
    for s in range(PACKED_SUBLANES):
        h2_ref[pl.ds(s, tm, stride=PACKED_SUBLANES), :] = words[:, LANES * s:LANES * (s + 1)]
    shared = _dot(_silu(_dot(h2, wsg_ref[...])) * _dot(h2, wsu_ref[...]), wsd_ref[...])
    pre = h1 + gt2 * shared
    for s in range(SUBLANES):
        pre_ref[pl.ds(s, tm, stride=SUBLANES), :] = pre[:, LANES * s:LANES * (s + 1)]
    scores = jax.nn.sigmoid(_dot_nt(wrt_ref[...], h2))
    biased = scores + rb_ref[...]
    E = N_EXPERTS
    per = E // N_GROUPS
    rowp = lax.broadcasted_iota(I32, (per, tm), 0)
    gs, vals = [], []
    for g in range(N_GROUPS):
        v = biased[per * g:per * (g + 1), :]
        m1 = jnp.max(v, axis=0, keepdims=True)
        i1 = jnp.min(jnp.where(v == m1, rowp, per), axis=0, keepdims=True)
        m2 = jnp.max(jnp.where(rowp == i1, -jnp.inf, v), axis=0, keepdims=True)
        gs.append(m1 + m2)
        vals.append(v)
    cands = []
    for g in range(N_GROUPS):
        rank = jnp.zeros((1, tm), I32)
        for i in range(N_GROUPS):
            beats = (gs[i] > gs[g]) | ((gs[i] == gs[g]) & (i < g))
            rank = rank + beats.astype(I32)
        cands.append(jnp.where(rank < TOPK_GROUPS, vals[g], -jnp.inf))
    cand = jnp.concatenate(cands, axis=0)
    row = lax.broadcasted_iota(I32, (E, tm), 0)
    krow = lax.broadcasted_iota(I32, (TOP_K, tm), 0)
    eidx = jnp.zeros((TOP_K, tm), I32)
    wsel = jnp.zeros((TOP_K, tm), F32)
    hot = jnp.zeros((E, tm), F32)
    wsum = jnp.zeros((1, tm), F32)
    for kk in range(TOP_K):
        m = jnp.max(cand, axis=0, keepdims=True)
        idx = jnp.min(jnp.where(cand == m, row, E), axis=0, keepdims=True)
        pick = row == idx
        w = jnp.sum(jnp.where(pick, scores, 0.0), axis=0, keepdims=True)
        cand = jnp.where(pick, -jnp.inf, cand)
        hot = hot + pick.astype(F32)
        eidx = jnp.where(krow == kk, idx, eidx)
        wsel = jnp.where(krow == kk, w, wsel)
        wsum = wsum + w
    eidx_ref[...] = eidx
    wts_ref[...] = wsel / wsum * ROUTED_SCALE
    hot_ref[...] = hot.astype(BF16)

    @pl.when(first)
    def _():
        cnt_ref[...] = jnp.zeros(cnt_ref.shape, F32)

    cnt_ref[...] += jnp.sum(hot, axis=1, keepdims=True)


def _post_attn(x, mod3, o_cmp, o_slc, o_win, gates, o_b, w_out, g_ffn, w_router, router_bias, w_sg, w_su, w_sd):
    B, S, D = x.shape
    tm = SEQ_TILE
    nt = S // tm
    T = B * S
    E = N_EXPERTS
    tok = lambda w_: pl.BlockSpec((1, tm, w_), lambda b, i: (b, i, 0))
    full = lambda a: pl.BlockSpec(a.shape, lambda b, i: (0,) * a.ndim)
    rows = lambda r_: pl.BlockSpec((tm * r_, LANES), lambda b, i: (b * nt + i, 0))
    cols = lambda r_: pl.BlockSpec((r_, tm), lambda b, i: (0, b * nt + i))
    consts = (w_out.astype(BF16), g_ffn.reshape(1, D), w_router.T.astype(BF16), router_bias.reshape(E, 1),
              w_sg.astype(BF16), w_su.astype(BF16), w_sd.astype(BF16))
    return pl.pallas_call(
        _post_attn_kernel,
        grid=(B, nt),
        in_specs=[tok(D), pl.BlockSpec((1, 6, D), lambda b, i: (b, 0, 0)),
                  tok(512), tok(512), tok(512), tok(LANES), tok(512)] + [full(a) for a in consts],
        out_specs=[rows(PACKED_SUBLANES), rows(SUBLANES), cols(TOP_K), cols(TOP_K), cols(E),
                   pl.BlockSpec((E, 1), lambda b, i: (0, 0))],
        out_shape=[jax.ShapeDtypeStruct((T * PACKED_SUBLANES, LANES), U32),
                   jax.ShapeDtypeStruct((T * SUBLANES, LANES), F32),
                   jax.ShapeDtypeStruct((TOP_K, T), I32), jax.ShapeDtypeStruct((TOP_K, T), F32),
                   jax.ShapeDtypeStruct((E, T), BF16), jax.ShapeDtypeStruct((E, 1), F32)],
        compiler_params=_cparams(("arbitrary", "arbitrary")),
        name="post_attn_router",
    )(x, mod3, o_cmp, o_slc, o_win, gates, o_b, *consts)


def _moe_pos_kernel(hot_ref, eidx_ref, start_ref, dest_ref, carry_ref):
    @pl.when(pl.program_id(0) == 0)
    def _():
        carry_ref[...] = start_ref[...]

    hot = hot_ref[...]
    E, tp = hot.shape
    r = lax.broadcasted_iota(I32, (tp, tp), 0)
    c = lax.broadcasted_iota(I32, (tp, tp), 1)
    earlier = (r < c).astype(BF16)
    pos = carry_ref[...] + _dot(hot, earlier)
    row = lax.broadcasted_iota(I32, (E, tp), 0)
    krow = lax.broadcasted_iota(I32, (TOP_K, tp), 0)
    eidx = eidx_ref[...]
    dest = jnp.zeros((TOP_K, tp), I32)
    for kk in range(TOP_K):
        d = jnp.sum(jnp.where(row == eidx[kk:kk + 1, :], pos, 0.0), axis=0, keepdims=True)
        dest = jnp.where(krow == kk, d.astype(I32), dest)
    dest_ref[...] = dest
    carry_ref[...] += jnp.sum(hot.astype(F32), axis=1, keepdims=True)


def _moe_pos(hot, eidx, starts):
    E, T = hot.shape
    tp = 512 if T % 512 == 0 else SEQ_TILE
    return pl.pallas_call(
        _moe_pos_kernel,
        grid=(T // tp,),
        in_specs=[pl.BlockSpec((E, tp), lambda i: (0, i)),
                  pl.BlockSpec((TOP_K, tp), lambda i: (0, i)),
                  pl.BlockSpec((E, 1), lambda i: (0, 0))],
        out_specs=pl.BlockSpec((TOP_K, tp), lambda i: (0, i)),
        out_shape=jax.ShapeDtypeStruct((TOP_K, T), I32),
        scratch_shapes=[pltpu.VMEM((E, 1), F32)],
        compiler_params=_cparams(("arbitrary",)),
        name="moe_pos",
    )(hot, eidx, starts)


def _row(ref, start, n=SUBLANES):
    return ref.at[pl.ds(pl.multiple_of(start, n), n)]


def _moe_scatter_kernel(dest_ref, last_ref, h_ref, xs_ref, zbuf, sem, zsem):
    n = PACKED_SUBLANES
    ts = h_ref.shape[0] // n

    @pl.when(pl.program_id(0) == 0)
    def _():
        zbuf[...] = jnp.zeros(zbuf.shape, U32)

        def zero_copy(e):
            return pltpu.make_async_copy(zbuf, xs_ref.at[pl.ds(pl.multiple_of(last_ref[e], EXPERT_ROWS),
                                                               EXPERT_ROWS)], zsem)

        def zstart(e, carry):
            @pl.when(last_ref[e] >= 0)
            def _():
                zero_copy(e).start()
            return carry

        def zwait(e, carry):
            @pl.when(last_ref[e] >= 0)
            def _():
                zero_copy(e).wait()
            return carry

        lax.fori_loop(0, N_EXPERTS, zstart, 0)
        lax.fori_loop(0, N_EXPERTS, zwait, 0)

    def copy(r, kk):
        return pltpu.make_async_copy(_row(h_ref, r * n, n), xs_ref.at[dest_ref[r * TOP_K + kk]], sem)

    def issue(r, carry):
        for kk in range(TOP_K):
            copy(r, kk).start(priority=kk % 2)
        return carry

    def drain(r, carry):
        for kk in range(TOP_K):
            copy(r, kk).wait()
        return carry

    lax.fori_loop(0, ts, issue, 0)
    lax.fori_loop(0, ts, drain, 0)


def _moe_scatter(h2p, dest, last_block_row, n_rows):
    n = PACKED_SUBLANES
    T = h2p.shape[0] // n
    ts = MOE_TILE
    return pl.pallas_call(
        _moe_scatter_kernel,
        grid=(T // ts,),
        in_specs=[pl.BlockSpec((ts * TOP_K,), lambda i: (i,), memory_space=pltpu.SMEM),
                  pl.BlockSpec((N_EXPERTS,), lambda i: (0,), memory_space=pltpu.SMEM),
                  pl.BlockSpec((ts * n, LANES), lambda i: (i, 0))],
        out_specs=pl.BlockSpec(memory_space=pl.ANY),
        out_shape=jax.ShapeDtypeStruct((n_rows, n, LANES), U32),
        scratch_shapes=[pltpu.VMEM((EXPERT_ROWS, n, LANES), U32), pltpu.SemaphoreType.DMA(()),
                        pltpu.SemaphoreType.DMA(())],
        compiler_params=_cparams(("arbitrary",)),
        name="moe_scatter",
    )(dest, last_block_row, h2p)


def _moe_experts_kernel(be_ref, bf_ref, nx_ref, wp_ref, nu_ref, xs_ref, wg_ref, wu_ref, wd_ref, ys_ref,
                        wg_s, wu_s, wd_s, wg_f, wu_f, wd_f, xbuf, ybuf, sem_in, sem_out, sem_w):
    i = pl.program_id(0)
    n_used = nu_ref[0]
    rows = EXPERT_ROWS
    slot = i % 2
    islot = i % IN_SLOTS

    def w_copies(e, sl):
        return [pltpu.make_async_copy(src.at[e], dst.at[sl], sem_w.at[sl])
                for src, dst in ((wg_ref, wg_f), (wu_ref, wu_f), (wd_ref, wd_f))]

    def in_copies(blk):
        r0 = pl.multiple_of(blk * rows, rows)
        sl = blk % IN_SLOTS
        return [pltpu.make_async_copy(xs_ref.at[pl.ds(r0, rows), s], xbuf.at[sl, s], sem_in.at[sl])
                for s in range(PACKED_SUBLANES)]

    def out_copies(blk, sl):
        r0 = pl.multiple_of(blk * rows, rows)
        return [pltpu.make_async_copy(ybuf.at[sl, s], ys_ref.at[pl.ds(r0, rows), s], sem_out.at[sl])
                for s in range(SUBLANES)]

    @pl.when(i == 0)
    def _():
        for cp in w_copies(be_ref[0], 0):
            cp.start()
        for ahead in range(IN_SLOTS - 1):
            @pl.when(ahead < n_used)
            def _():
                for cp in in_copies(ahead):
                    cp.start()

    @pl.when(i + IN_SLOTS - 1 < n_used)
    def _():
        for cp in in_copies(i + IN_SLOTS - 1):
            cp.start()

    @pl.when((bf_ref[i] == 1) & (i < n_used))
    def _():
        wslot = wp_ref[i]
        for cp in w_copies(be_ref[i], wslot):
            cp.wait()
        wg_s[...] = wg_f[wslot].astype(BF16)
        wu_s[...] = wu_f[wslot].astype(BF16)
        wd_s[...] = wd_f[wslot].astype(BF16)

        @pl.when(nx_ref[i] >= 0)
        def _():
            for cp in w_copies(nx_ref[i], 1 - wslot):
                cp.start()

    @pl.when(i < n_used)
    def _():
        for cp in in_copies(i):
            cp.wait()

        @pl.when(i >= 2)
        def _():
            for cp in out_copies(i - 2, slot):
                cp.wait()

        words = [xbuf[islot, s] for s in range(PACKED_SUBLANES)]
        lo = [lax.bitcast_convert_type(w << 16, F32).astype(BF16) for w in words]
        hi = [lax.bitcast_convert_type(w & jnp.uint32(0xFFFF0000), F32).astype(BF16) for w in words]
        x = jnp.concatenate(lo + hi, axis=1)
        a = _silu(_dot(x, wg_s[...])) * _dot(x, wu_s[...])
        y = _dot(a, wd_s[...])
        for s in range(SUBLANES):
            ybuf[slot, s] = y[:, LANES * s:LANES * (s + 1)]
        for cp in out_copies(i, slot):
            cp.start()

        @pl.when(i == n_used - 1)
        def _():
            for cp in out_copies(i, slot):
                cp.wait()

            @pl.when(i >= 1)
            def _():
                for cp in out_copies(i - 1, 1 - slot):
                    cp.wait()


def _moe_experts(xs, blk_exp, blk_first, blk_next, blk_wslot, n_used, w_eg, w_eu, w_ed):
    n_rows = xs.shape[0]
    E, D, De = w_eg.shape
    n_blk = n_rows // EXPERT_ROWS
    grid_spec = pltpu.PrefetchScalarGridSpec(
        num_scalar_prefetch=5,
        grid=(n_blk,),
        in_specs=[pl.BlockSpec(memory_space=pl.ANY)] * 4,
        out_specs=pl.BlockSpec(memory_space=pl.ANY),
        scratch_shapes=[pltpu.VMEM((D, De), BF16), pltpu.VMEM((D, De), BF16), pltpu.VMEM((De, D), BF16),
                        pltpu.VMEM((2, D, De), F32), pltpu.VMEM((2, D, De), F32), pltpu.VMEM((2, De, D), F32),
                        pltpu.VMEM((IN_SLOTS, PACKED_SUBLANES, EXPERT_ROWS, LANES), U32),
                        pltpu.VMEM((2, SUBLANES, EXPERT_ROWS, LANES), F32),
                        pltpu.SemaphoreType.DMA((IN_SLOTS,)), pltpu.SemaphoreType.DMA((2,)),
                        pltpu.SemaphoreType.DMA((2,))],
    )
    return pl.pallas_call(
        _moe_experts_kernel,
        grid_spec=grid_spec,
        out_shape=jax.ShapeDtypeStruct((n_rows, SUBLANES, LANES), F32),
        compiler_params=_cparams(("arbitrary",)),
        name="moe_experts",
    )(blk_exp, blk_first, blk_next, blk_wslot, n_used, xs, w_eg, w_eu, w_ed)


def _moe_combine_kernel(dcur_ref, dnxt_ref, w_ref, pre_ref, gt2_ref, gfin_ref, ys_ref, o_ref, buf, hbuf, sem):
    i = pl.program_id(0)
    n = pl.num_programs(0)
    tc = pre_ref.shape[0] // SUBLANES
    slot = i % 2

    def copy(d_ref, sl, j):
        return pltpu.make_async_copy(ys_ref.at[d_ref[j]], _row(buf.at[sl], j * SUBLANES), sem.at[sl])

    def issue(d_ref, sl):
        def body(r2, carry):
            for u in range(2 * TOP_K):
                copy(d_ref, sl, r2 * (2 * TOP_K) + u).start(priority=u % 2)
            return carry
        lax.fori_loop(0, tc // 2, body, 0)

    @pl.when(i == 0)
    def _():
        issue(dcur_ref, 0)

    @pl.when(i + 1 < n)
    def _():
        issue(dnxt_ref, 1 - slot)

    def drain(r, carry):
        for kk in range(TOP_K):
            copy(dcur_ref, slot, r * TOP_K + kk).wait()
        return carry

    lax.fori_loop(0, tc, drain, 0)
    cur = buf.at[slot]
    gt2 = gt2_ref[...]
    unroll = 4

    def tokens(g, carry):
        for u in range(unroll):
            r = g * unroll + u
            routed = jnp.zeros((SUBLANES, LANES), F32)
            for kk in range(TOP_K):
                j = r * TOP_K + kk
                routed = routed + w_ref[j] * _row(cur, j * SUBLANES)[...]
            _row(hbuf, r * SUBLANES)[...] = _row(pre_ref, r * SUBLANES)[...] + gt2 * routed
        return carry

    lax.fori_loop(0, tc // unroll, tokens, 0)
    h = jnp.concatenate([hbuf[pl.ds(s, tc, stride=SUBLANES), :] for s in range(SUBLANES)], axis=1)
    o_ref[...] = _rms(h, gfin_ref[...])


def _moe_combine(ys, dest8, w_flat, pre2d, gt2_2d, gfin, S):
    T = pre2d.shape[0] // SUBLANES
    tc = MOE_TILE
    per_b = S // tc
    n = T // tc
    return pl.pallas_call(
        _moe_combine_kernel,
        grid=(n,),
        in_specs=[pl.BlockSpec((tc * TOP_K,), lambda i: (i,), memory_space=pltpu.SMEM),
                  pl.BlockSpec((tc * TOP_K,), lambda i: (jnp.minimum(i + 1, n - 1),), memory_space=pltpu.SMEM),
                  pl.BlockSpec((tc * TOP_K,), lambda i: (i,), memory_space=pltpu.SMEM),
                  pl.BlockSpec((tc * SUBLANES, LANES), lambda i: (i, 0)),
                  pl.BlockSpec((SUBLANES, LANES), lambda i: (i // per_b, 0)),
                  pl.BlockSpec((1, SUBLANES * LANES), lambda i: (0, 0)),
                  pl.BlockSpec(memory_space=pl.ANY)],
        out_specs=pl.BlockSpec((tc, SUBLANES * LANES), lambda i: (i, 0)),
        out_shape=jax.ShapeDtypeStruct((T, SUBLANES * LANES), F32),
        scratch_shapes=[pltpu.VMEM((2, tc * TOP_K * SUBLANES, LANES), F32),
                        pltpu.VMEM((tc * SUBLANES, LANES), F32),
                        pltpu.SemaphoreType.DMA((2,))],
        compiler_params=_cparams(("arbitrary",)),
        name="moe_combine",
    )(dest8, dest8, w_flat, pre2d, gt2_2d, gfin, ys)


def _layer(h_res, mod, g_attn, w_in, cmp_pos_k, cmp_w1_k, cmp_w2_k, cmp_pos_v, cmp_w1_v, cmp_w2_v,
           w_out, g_ffn, w_router, router_bias, w_eg, w_eu, w_ed, w_sg, w_su, w_sd):
    B, S, D = h_res.shape
    T = B * S
    mod3 = mod.reshape(B, 6, D)
    (qa, kc, vc, ks, kw, gates, qb, kb, kmean, vb_t, vs_t, vw_t) = _in_proj(h_res, mod3, g_attn, w_in)
    kcmp, vcmp_t = _compress(kc, vc, cmp_pos_k, cmp_w1_k, cmp_w2_k, cmp_pos_v, cmp_w1_v, cmp_w2_v)
    o_cmp, sel_t = _nsa_cmp(qa, kcmp, vcmp_t)
    o_slc = _nsa_flash("slc", qa, ks, vs_t, sel_t)
    o_win = _nsa_flash("win", qa, kw, vw_t)
    o_b = _moba(qb, kb, vb_t, kmean.reshape(B, S // MOBA_BLOCK, 512))
    h2p, pre2d, eidx, wts, hot, counts = _post_attn(h_res, mod3, o_cmp, o_slc, o_win, gates, o_b, w_out, g_ffn,
                                                    w_router, router_bias, w_sg, w_su, w_sd)
    counts = counts.reshape(N_EXPERTS).astype(I32)
    padded = (counts + EXPERT_ROWS - 1) // EXPERT_ROWS * EXPERT_ROWS
    ends = jnp.cumsum(padded)
    starts = ends - padded
    n_blk = -(-(T * TOP_K) // EXPERT_ROWS) + N_EXPERTS
    n_rows = n_blk * EXPERT_ROWS
    blk_row = jnp.arange(n_blk, dtype=I32) * EXPERT_ROWS
    blk_exp = jnp.minimum(jnp.sum((ends[None, :] <= blk_row[:, None]).astype(I32), axis=1), N_EXPERTS - 1)
    blk_first = jnp.concatenate([jnp.ones((1,), I32), (blk_exp[1:] != blk_exp[:-1]).astype(I32)])
    n_used = (ends[-1:] // EXPERT_ROWS).astype(I32)
    blk_idx = jnp.arange(n_blk, dtype=I32)
    first_used = (blk_first == 1) & (blk_idx < n_used[0])
    first_pos = jnp.where(first_used, blk_idx, n_blk)
    next_pos = jnp.concatenate([lax.cummin(first_pos, reverse=True)[1:], jnp.full((1,), n_blk, I32)])
    blk_next = jnp.where(next_pos < n_blk, blk_exp[jnp.minimum(next_pos, n_blk - 1)], -1).astype(I32)
    blk_wslot = ((jnp.cumsum(first_used.astype(I32)) - 1) % 2).astype(I32)
    dest = _moe_pos(hot, eidx, starts.astype(F32).reshape(N_EXPERTS, 1))
    dest_flat = dest.T.reshape(T * TOP_K)
    w_flat = wts.T.reshape(T * TOP_K)
    last_block_row = jnp.where(counts > 0, ends - EXPERT_ROWS, -1).astype(I32)
    xs = _moe_scatter(h2p, dest_flat, last_block_row, n_rows)
    ys = _moe_experts(xs, blk_exp, blk_first, blk_next, blk_wslot, n_used, w_eg, w_eu, w_ed)
    gt2_2d = mod3[:, 5, :].reshape(B * SUBLANES, LANES)
    return ys, dest_flat, w_flat, pre2d, gt2_2d


def kernel(x, c, w_ada, b_ada, g_attn, w_in, cmp_pos_k, cmp_w1_k, cmp_w2_k, cmp_pos_v, cmp_w1_v, cmp_w2_v, w_out, g_ffn, w_router, router_bias, w_exp_gate, w_exp_up, w_exp_down, w_sh_gate, w_sh_up, w_sh_down, g_final):
    B, S, D = x.shape
    depth = w_ada.shape[0]
    assert depth == 1, "the final RMSNorm is fused into the single layer's MoE combine"
    assert D == SUBLANES * LANES and S % SEQ_TILE == 0
    l = 0
    mod = _ada_mod(c.astype(F32), w_ada[l], b_ada[l])
    ys, dest8, w_flat, pre2d, gt2_2d = _layer(
        x.astype(F32), mod, g_attn[l], w_in[l], cmp_pos_k[l], cmp_w1_k[l], cmp_w2_k[l], cmp_pos_v[l], cmp_w1_v[l],
        cmp_w2_v[l], w_out[l], g_ffn[l], w_router[l], router_bias[l], w_exp_gate[l], w_exp_up[l], w_exp_down[l],
        w_sh_gate[l], w_sh_up[l], w_sh_down[l])
    out = _moe_combine(ys, dest8, w_flat, pre2d, gt2_2d, g_final.reshape(1, D).astype(F32), S)
    return out.reshape(B, S, D).astype(x.dtype)
```
